```python
import jax, jax.numpy as jnp
from jax import lax
import numpy as np

D_MODEL = 2048
BATCH = 4
SEQ = 2048
DEPTH = 4
DEC_BATCH = 8
DEC_SEQ = 4
PAST_LEN = 16384
PAGE_SIZE = 128

N_A_LAYERS = DEPTH // 2
N_B_LAYERS = DEPTH - N_A_LAYERS
POOL_WINDOWS = (2, 4, 8, 16)
N_POOL_GROUPS = len(POOL_WINDOWS)
POOL_GROUP = D_MODEL // N_POOL_GROUPS
POOL_STATE = max(POOL_WINDOWS) - 1
HEAD_DIM = 128
N_HEADS = D_MODEL // HEAD_DIM
D_FF = ((8 * D_MODEL // 3 + 127) // 128) * 128
Q_BLOCK = 128
N_MOD = 9
SB_BIAS_INIT = -7.0
EPS = 1e-6

kernel_name = "yoco_pool_stickbreaking_step"


def rmsnorm(x, g):
    xf = x.astype(jnp.float32)
    y = xf * lax.rsqrt(jnp.mean(xf * xf, axis=-1, keepdims=True) + EPS)
    return (y * g.astype(jnp.float32)).astype(x.dtype)


def modulate(h, shift, scale):
    return h * (1 + scale) + shift


def swiglu(h, wg, wu, wd):
    return (jax.nn.silu(h @ wg) * (h @ wu)) @ wd


def pool_mixer(h, prev, pos0, w_pool, scale):
    B, T, _ = h.shape
    P = prev.shape[1]
    full = jnp.concatenate([prev.astype(h.dtype), h], axis=1)
    cs = jnp.cumsum(full.astype(jnp.float32), axis=1)
    cs = jnp.concatenate([jnp.zeros((B, 1, D_MODEL), jnp.float32), cs], axis=1)
    end = cs[:, P + 1:P + 1 + T]
    pos = (pos0 + jnp.arange(T)).astype(jnp.float32)
    outs = []
    for g, w in enumerate(POOL_WINDOWS):
        sl = slice(g * POOL_GROUP, (g + 1) * POOL_GROUP)
        wsum = end[..., sl] - cs[:, P + 1 - w:P + 1 - w + T, sl]
        cnt = jnp.minimum(float(w), pos + 1.0)[None, :, None]
        diff = (wsum / cnt).astype(h.dtype) - h[..., sl]
        outs.append(diff @ w_pool[g])
    out = jnp.concatenate(outs, axis=-1) * scale
    return out, full[:, -POOL_STATE:]


def sb_block(q, k, v, bias, q_pos, k_pos):
    z = jnp.einsum('bqhd,bkhd->bhqk', q, k).astype(jnp.float32) * (HEAD_DIM ** -0.5)
    z = z + bias.astype(jnp.float32)[None, :, None, None]
    mask = (k_pos[None, :] < q_pos[:, None])[None, None]
    log_keep = jnp.where(mask, jax.nn.log_sigmoid(-z), 0.0)
    after = lax.cumsum(log_keep, axis=3, reverse=True) - log_keep
    a = jnp.where(mask, jnp.exp(jax.nn.log_sigmoid(z) + after), 0.0)
    return jnp.einsum('bhqk,bkhd->bqhd', a.astype(v.dtype), v)


def sb_attend(q, k, v, bias, q_pos, k_pos):
    B, Tq, H, hd = q.shape
    qb = Q_BLOCK if Tq % Q_BLOCK == 0 else Tq
    nb = Tq // qb
    qs = q.reshape(B, nb, qb, H, hd).transpose(1, 0, 2, 3, 4)
    ps = q_pos.reshape(nb, qb)
    out = lax.map(lambda a: sb_block(a[0], k, v, bias, a[1], k_pos), (qs, ps))
    return out.transpose(1, 0, 2, 3, 4).reshape(B, Tq, H, hd)


def trunk(x, c, pos0, pool_prev, k_past, v_past, norm_g, w_ada, b_ada, w_gate, w_up, w_down,
          w_pool, pool_scale, kv_norm_g, w_ada_kv, b_ada_kv, w_kv, k_norm_g, w_q, q_norm_g,
          sb_bias, w_o):
    B, T, _ = x.shape
    sc = jax.nn.silu(c)
    q_pos = pos0 + jnp.arange(T)
    pool_new = []
    k_new = v_new = k_all = v_all = k_pos = None
    for l in range(DEPTH):
        mods = jnp.split((sc @ w_ada[l] + b_ada[l])[:, None, :], N_MOD, axis=-1)
        h = modulate(rmsnorm(x, norm_g[l, 0]), mods[0], mods[1])
        x = x + 0.5 * mods[2] * swiglu(h, w_gate[l, 0], w_up[l, 0], w_down[l, 0])
        h = modulate(rmsnorm(x, norm_g[l, 1]), mods[3], mods[4])
        if l < N_A_LAYERS:
            m, st = pool_mixer(h, pool_prev[l], pos0, w_pool[l], pool_scale[l])
            pool_new.append(st)
        else:
            i = l - N_A_LAYERS
            q = rmsnorm((h @ w_q[i]).reshape(B, T, N_HEADS, HEAD_DIM), q_norm_g[i])
            o = sb_attend(q, k_all, v_all, sb_bias[i], q_pos, k_pos)
            m = o.reshape(B, T, D_MODEL) @ w_o[i]
        x = x + mods[5] * m
        h = modulate(rmsnorm(x, norm_g[l, 2]), mods[6], mods[7])
        x = x + 0.5 * mods[8] * swiglu(h, w_gate[l, 1], w_up[l, 1], w_down[l, 1])
        if l == N_A_LAYERS - 1:
            kv_shift, kv_scale = jnp.split((sc @ w_ada_kv + b_ada_kv)[:, None, :], 2, axis=-1)
            hk = modulate(rmsnorm(x, kv_norm_g), kv_shift, kv_scale)
            kn, vn = jnp.split(hk @ w_kv, 2, axis=-1)
            k_new = rmsnorm(kn.reshape(B, T, N_HEADS, HEAD_DIM), k_norm_g)
            v_new = vn.reshape(B, T, N_HEADS, HEAD_DIM)
            if k_past is None:
                k_all, v_all, k_pos = k_new, v_new, q_pos
            else:
                k_all = jnp.concatenate([k_past.astype(k_new.dtype), k_new], axis=1)
                v_all = jnp.concatenate([v_past.astype(v_new.dtype), v_new], axis=1)
                k_pos = jnp.arange(pos0 + T)
    return x, k_new, v_new, jnp.stack(pool_new)


def setup_inputs(seed: int = 0) -> dict:
    key = jax.random.key(seed)
    ks = jax.random.split(key, 25)
    nrm = jax.random.normal
    D = D_MODEL
    n_pages = PAST_LEN // PAGE_SIZE
    n_used = DEC_BATCH * n_pages
    n_phys = n_used + n_used // 4
    page_table = jax.random.permutation(ks[7], n_phys)[:n_used].reshape(DEC_BATCH, n_pages).astype(jnp.int32)
    return {
        "x_prompt": nrm(ks[0], (BATCH, SEQ, D)),
        "x_sample": nrm(ks[1], (DEC_BATCH, DEC_SEQ, D)),
        "c_prompt": nrm(ks[2], (BATCH, D)),
        "c_sample": nrm(ks[3], (DEC_BATCH, D)),
        "cache_k": nrm(ks[4], (n_phys, PAGE_SIZE, N_HEADS, HEAD_DIM)),
        "cache_v": nrm(ks[5], (n_phys, PAGE_SIZE, N_HEADS, HEAD_DIM)),
        "state_pool": nrm(ks[6], (N_A_LAYERS, DEC_BATCH, POOL_STATE, D)),
        "page_table": page_table,
        "norm_g": 1.0 + 0.02 * nrm(ks[8], (DEPTH, 3, D)),
        "w_ada": nrm(ks[9], (DEPTH, D, N_MOD * D)) * (0.5 * D ** -0.5),
        "b_ada": 0.01 * nrm(ks[10], (DEPTH, N_MOD * D)),
        "w_gate": nrm(ks[11], (DEPTH, 2, D, D_FF)) * D ** -0.5,
        "w_up": nrm(ks[12], (DEPTH, 2, D, D_FF)) * D ** -0.5,
        "w_down": nrm(ks[13], (DEPTH, 2, D_FF, D)) * D_FF ** -0.5,
        "w_pool": nrm(ks[14], (N_A_LAYERS, N_POOL_GROUPS, POOL_GROUP, POOL_GROUP)) * POOL_GROUP ** -0.5,
        "pool_scale": 1.0 + 0.1 * nrm(ks[15], (N_A_LAYERS, D)),
        "kv_norm_g": 1.0 + 0.02 * nrm(ks[16], (D,)),
        "w_ada_kv": nrm(ks[17], (D, 2 * D)) * (0.5 * D ** -0.5),
        "b_ada_kv": 0.01 * nrm(ks[18], (2 * D,)),
        "w_kv": nrm(ks[19], (D, 2 * D)) * D ** -0.5,
        "k_norm_g": 1.0 + 0.02 * nrm(ks[20], (HEAD_DIM,)),
        "w_q": nrm(ks[21], (N_B_LAYERS, D, D)) * D ** -0.5,
        "q_norm_g": 1.0 + 0.02 * nrm(ks[22], (N_B_LAYERS, HEAD_DIM)),
        "sb_bias": SB_BIAS_INIT + 0.1 * nrm(ks[24], (N_B_LAYERS, N_HEADS)),
        "w_o": nrm(ks[23], (N_B_LAYERS, D, D)) * D ** -0.5,
    }


def reference(x_prompt, x_sample, c_prompt, c_sample, cache_k, cache_v, state_pool, page_table,
              norm_g, w_ada, b_ada, w_gate, w_up, w_down, w_pool, pool_scale, kv_norm_g,
              w_ada_kv, b_ada_kv, w_kv, k_norm_g, w_q, q_norm_g, sb_bias, w_o):
    weights = (norm_g, w_ada, b_ada, w_gate, w_up, w_down, w_pool, pool_scale, kv_norm_g,
               w_ada_kv, b_ada_kv, w_kv, k_norm_g, w_q, q_norm_g, sb_bias, w_o)
    pool_zero = jnp.zeros((N_A_LAYERS, x_prompt.shape[0], POOL_STATE, D_MODEL), x_prompt.dtype)
    y_prompt, k_prompt, v_prompt, pool_prompt = trunk(
        x_prompt, c_prompt, 0, pool_zero, None, None, *weights)
    nb, n_pages = page_table.shape
    past = n_pages * cache_k.shape[1]
    k_past = cache_k[page_table].reshape(nb, past, N_HEADS, HEAD_DIM)
    v_past = cache_v[page_table].reshape(nb, past, N_HEADS, HEAD_DIM)
    y_sample, k_sample, v_sample, pool_sample = trunk(
        x_sample, c_sample, past, state_pool, k_past, v_past, *weights)
    return (y_prompt, y_sample, k_prompt, v_prompt, k_sample, v_sample, pool_prompt, pool_sample)
```

```python
import functools

import jax
import jax.numpy as jnp
from jax import lax
from jax.experimental import pallas as pl
from jax.experimental.pallas import tpu as pltpu

V7X_LANES = 128
V7X_SUBLANES = 8
V7X_VMEM_BYTES = 64 * 1024 * 1024

POOL_WINDOWS = (2, 4, 8, 16)
POOL_STATE = max(POOL_WINDOWS) - 1
POOL_HALO = POOL_STATE + 1
HEAD_DIM = 128
N_MOD = 9
EPS = 1e-6

BF16 = jnp.bfloat16
F32 = jnp.float32


def _dot(a, b):
    return jnp.dot(a, b, preferred_element_type=F32)


def _dot_nt(a, b):
    return lax.dot_general(a, b, (((1,), (1,)), ((), ())), preferred_element_type=F32)


def _norm_mod(x, g, shift, scale):
    ms = jnp.mean(x * x, axis=-1, keepdims=True)
    y = x * lax.rsqrt(ms + EPS) * g
    return y * (1.0 + scale) + shift


def _params(sem, vmem_mib):
    return pltpu.CompilerParams(dimension_semantics=sem,
                                vmem_limit_bytes=vmem_mib * 1024 * 1024)


def _mod_spec(mods, j, tm, rows_per_batch, width, ncol=1):
    if ncol == 1:
        col = lambda g: j
    else:
        col = lambda g: j * ncol + g[1]
    if mods.ndim == 3:
        tiles_per_batch = rows_per_batch // tm
        return pl.BlockSpec((None, 1, width), lambda *g: (g[0] // tiles_per_batch, 0, col(g)))
    return pl.BlockSpec((tm, width), lambda *g: (g[0], col(g)))


def _ada_kernel(c_ref, w_ref, b_ref, o_ref):
    c = c_ref[...]
    sc = (c * jax.nn.sigmoid(c)).astype(BF16)
    o_ref[...] = _dot(sc, w_ref[...].astype(BF16)) + b_ref[...]


def _ada(c, w, b, tn):
    L, D, N = w.shape
    R = c.shape[0]
    return pl.pallas_call(
        _ada_kernel,
        grid=(L, N // tn),
        in_specs=[
            pl.BlockSpec((R, D), lambda l, n: (0, 0)),
            pl.BlockSpec((None, D, tn), lambda l, n: (l, 0, n)),
            pl.BlockSpec((None, 1, tn), lambda l, n: (l, 0, n)),
        ],
        out_specs=pl.BlockSpec((None, R, tn), lambda l, n: (l, 0, n)),
        out_shape=jax.ShapeDtypeStruct((L, R, N), F32),
        compiler_params=_params(("arbitrary", "arbitrary"), 40),
        name="ada",
    )(c, w, b.reshape(L, 1, N))


def _ffn_kernel(x_ref, g_ref, sh_ref, sc_ref, gt_ref, wg_ref, wu_ref, wd_ref, o_ref, h_ref,
                *, nf, tf, last):
    f = pl.program_id(1)

    @pl.when(f == 0)
    def _():
        h_ref[...] = _norm_mod(x_ref[...], g_ref[...], sh_ref[...], sc_ref[...]).astype(BF16)
        o_ref[...] = jnp.zeros_like(o_ref)

    def step(width):
        h = h_ref[...]
        gate = _dot(h, wg_ref[:, :width].astype(BF16))
        up = _dot(h, wu_ref[:, :width].astype(BF16))
        a = (gate * jax.nn.sigmoid(gate) * up).astype(BF16)
        o_ref[...] += _dot(a, wd_ref[:width, :].astype(BF16))

    if last == tf:
        step(tf)
    else:
        pl.when(f < nf - 1)(lambda: step(tf))
        pl.when(f == nf - 1)(lambda: step(last))

    @pl.when(f == nf - 1)
    def _():
        o_ref[...] = x_ref[...] + 0.5 * gt_ref[...] * o_ref[...]


def _ffn(x, mods, midx, g, w_gate, w_up, w_down, l, j, *, tm, tf, rows_per_batch, vmem_mib):
    M, D = x.shape
    F = w_gate.shape[-1]
    nf = pl.cdiv(F, tf)
    last = F - (nf - 1) * tf
    ms = functools.partial(_mod_spec, mods, tm=tm, rows_per_batch=rows_per_batch, width=D)
    return pl.pallas_call(
        functools.partial(_ffn_kernel, nf=nf, tf=tf, last=last),
        grid=(M // tm, nf),
        in_specs=[
            pl.BlockSpec((tm, D), lambda i, f: (i, 0)),
            pl.BlockSpec((1, D), lambda i, f: (0, 0)),
            ms(j=midx[0]), ms(j=midx[1]), ms(j=midx[2]),
            pl.BlockSpec((None, None, D, tf), lambda i, f: (l, j, 0, f)),
            pl.BlockSpec((None, None, D, tf), lambda i, f: (l, j, 0, f)),
            pl.BlockSpec((None, None, tf, D), lambda i, f: (l, j, f, 0)),
        ],
        out_specs=pl.BlockSpec((tm, D), lambda i, f: (i, 0)),
        out_shape=jax.ShapeDtypeStruct((M, D), F32),
        scratch_shapes=[pltpu.VMEM((tm, D), BF16)],
        compiler_params=_params(("arbitrary", "arbitrary"), vmem_mib),
        name="ffn",
    )(x, g, mods, mods, mods, w_gate, w_up, w_down)


def _proj_kernel(x_ref, g_ref, sh_ref, sc_ref, w_ref, hg_ref, *rest, head_norm, n_out):
    outs, h_ref = rest[:n_out], rest[n_out]
    n = pl.program_id(1)

    @pl.when(n == 0)
    def _():
        h_ref[...] = _norm_mod(x_ref[...], g_ref[...], sh_ref[...], sc_ref[...]).astype(BF16)

    y = _dot(h_ref[...], w_ref[...].astype(BF16))
    if head_norm:
        hg = hg_ref[...]
        for c in range(y.shape[1] // HEAD_DIM):
            sl = slice(c * HEAD_DIM, (c + 1) * HEAD_DIM)
            yh = y[:, sl]
            ms = jnp.mean(yh * yh, axis=-1, keepdims=True)
            yn = yh * lax.rsqrt(ms + EPS) * hg
            for o in outs:
                o[:, sl] = yn.astype(o.dtype)
    else:
        for o in outs:
            o[...] = y.astype(o.dtype)


def _proj(x, mods, midx, g, w, wi, col0, n_cols, head_g, out_dtypes, *, tm, tn, rows_per_batch,
          vmem_mib):
    M, D = x.shape
    ms = functools.partial(_mod_spec, mods, tm=tm, rows_per_batch=rows_per_batch, width=D)
    cb = col0 // tn
    head_norm = head_g is not None
    if head_g is None:
        head_g = jnp.ones((HEAD_DIM,), F32)
    outs = pl.pallas_call(
        functools.partial(_proj_kernel, head_norm=head_norm, n_out=len(out_dtypes)),
        grid=(M // tm, n_cols // tn),
        in_specs=[
            pl.BlockSpec((tm, D), lambda i, n: (i, 0)),
            pl.BlockSpec((1, D), lambda i, n: (0, 0)),
            ms(j=midx[0]), ms(j=midx[1]),
            pl.BlockSpec((None, D, tn), lambda i, n: (wi, 0, cb + n)),
            pl.BlockSpec((1, HEAD_DIM), lambda i, n: (0, 0)),
        ],
        out_specs=[pl.BlockSpec((tm, tn), lambda i, n: (i, n)) for _ in out_dtypes],
        out_shape=[jax.ShapeDtypeStruct((M, n_cols), dt) for dt in out_dtypes],
        scratch_shapes=[pltpu.VMEM((tm, D), BF16)],
        compiler_params=_params(("arbitrary", "arbitrary"), vmem_mib),
        name="proj",
    )(x, g, mods, mods, w, head_g.reshape(1, HEAD_DIM))
    return outs


def _oproj_kernel(o_ref, w_ref, x_ref, gt_ref, y_ref):
    y_ref[...] = x_ref[...] + gt_ref[...] * _dot(o_ref[...], w_ref[...].astype(BF16))


def _oproj(o, x, mods, gidx, w, wi, *, tm, tn, rows_per_batch, vmem_mib):
    M, D = x.shape
    return pl.pallas_call(
        _oproj_kernel,
        grid=(M // tm, D // tn),
        in_specs=[
            pl.BlockSpec((tm, D), lambda i, n: (i, 0)),
            pl.BlockSpec((None, D, tn), lambda i, n: (wi, 0, n)),
            pl.BlockSpec((tm, tn), lambda i, n: (i, n)),
            _mod_spec(mods, gidx, tm, rows_per_batch, tn, ncol=D // tn),
        ],
        out_specs=pl.BlockSpec((tm, tn), lambda i, n: (i, n)),
        out_shape=jax.ShapeDtypeStruct((M, D), F32),
        compiler_params=_params(("arbitrary", "arbitrary"), vmem_mib),
        name="oproj",
    )(o, w, x, mods)


def _pool_kernel(x_ref, g_ref, sh_ref, sc_ref, gt_ref, prev_ref, w_ref, ps_ref, y_ref, st_ref,
                 buf_ref, *, tt, pos0):
    t = pl.program_id(1)
    D = x_ref.shape[-1]
    group = D // len(POOL_WINDOWS)

    @pl.when(t == 0)
    def _():
        buf_ref[0:POOL_HALO, :] = prev_ref[...]

    @pl.when(t > 0)
    def _():
        buf_ref[0:POOL_HALO, :] = buf_ref[tt:tt + POOL_HALO, :]

    x = x_ref[...]
    buf_ref[POOL_HALO:POOL_HALO + tt, :] = _norm_mod(x, g_ref[...], sh_ref[...], sc_ref[...])
    st_ref[...] = buf_ref[tt:tt + POOL_HALO, :]

    pos = (pos0 + t * tt + lax.broadcasted_iota(jnp.int32, (tt, 1), 0)).astype(F32)
    for gi, w in enumerate(POOL_WINDOWS):
        sl = slice(gi * group, (gi + 1) * group)
        h = buf_ref[POOL_HALO:POOL_HALO + tt, sl]
        wsum = h
        for back in range(1, w):
            wsum = wsum + buf_ref[POOL_HALO - back:POOL_HALO - back + tt, sl]
        cnt = jnp.minimum(float(w), pos + 1.0)
        diff = (wsum / cnt - h).astype(BF16)
        m = _dot(diff, w_ref[gi].astype(BF16)) * ps_ref[:, sl]
        y_ref[:, sl] = x[:, sl] + gt_ref[:, sl] * m


def _pool(x3, mods_b, midx, g, prev, w_pool_l, pool_scale_l, pos0, *, tt):
    B, T, D = x3.shape
    ms = lambda j: pl.BlockSpec((None, 1, D), lambda b, t: (b, 0, j))
    return pl.pallas_call(
        functools.partial(_pool_kernel, tt=tt, pos0=pos0),
        grid=(B, T // tt),
        in_specs=[
            pl.BlockSpec((None, tt, D), lambda b, t: (b, t, 0)),
            pl.BlockSpec((1, D), lambda b, t: (0, 0)),
            ms(midx[0]), ms(midx[1]), ms(midx[2]),
            pl.BlockSpec((None, POOL_HALO, D), lambda b, t: (b, 0, 0)),
            pl.BlockSpec(w_pool_l.shape, lambda b, t: (0, 0, 0)),
            pl.BlockSpec((1, D), lambda b, t: (0, 0)),
        ],
        out_specs=[
            pl.BlockSpec((None, tt, D), lambda b, t: (b, t, 0)),
            pl.BlockSpec((None, POOL_HALO, D), lambda b, t: (b, 0, 0)),
        ],
        out_shape=[jax.ShapeDtypeStruct((B, T, D), F32),
                   jax.ShapeDtypeStruct((B, POOL_HALO, D), F32)],
        scratch_shapes=[pltpu.VMEM((POOL_HALO + tt, D), F32)],
        compiler_params=_params(("arbitrary", "arbitrary"), 48),
        name="pool",
    )(x3, g, mods_b, mods_b, mods_b, prev, w_pool_l, pool_scale_l.reshape(1, D))


def _sb_terms(z):
    sp = jnp.maximum(z, 0.0) + jnp.log1p(jnp.exp(-jnp.abs(z)))
    return -sp, z - sp


def _split_bf16(x):
    hi = x.astype(BF16)
    lo = (x - hi.astype(F32)).astype(BF16)
    return hi, lo


def _attn_kernel(bias_ref, q_ref, k_ref, v_ref, o_ref, *, tq, scale):
    hd = pl.program_id(1)
    i = pl.program_id(2)
    bias = bias_ref[hd]
    q = q_ref[...]
    row = lax.broadcasted_iota(jnp.int32, (tq, tq), 0)
    col = lax.broadcasted_iota(jnp.int32, (tq, tq), 1)
    tri = jnp.where(row > col, 1.0, 0.0).astype(BF16)
    causal = col < row

    def tile(kb, vb, run, mask):
        z = _dot_nt(q, kb) * scale + bias
        lk, lb = _sb_terms(z)
        if mask is not None:
            lk = jnp.where(mask, lk, 0.0)
        hi, lo = _split_bf16(lk)
        after = _dot(hi, tri) + _dot(lo, tri) + run
        a = jnp.exp(lb + after)
        if mask is not None:
            a = jnp.where(mask, a, 0.0)
        return _dot(a.astype(BF16), vb), run + jnp.sum(lk, axis=-1, keepdims=True)

    d0 = pl.multiple_of(i * tq, tq)
    acc, run = tile(k_ref[pl.ds(d0, tq), :], v_ref[pl.ds(d0, tq), :],
                    jnp.zeros((tq, 1), F32), causal)

    def body(n, carry):
        acc, run = carry
        k0 = pl.multiple_of((i - 1 - n) * tq, tq)
        pv, run = tile(k_ref[pl.ds(k0, tq), :], v_ref[pl.ds(k0, tq), :], run, None)
        return acc + pv, run

    acc, run = lax.fori_loop(0, i, body, (acc, run))
    o_ref[...] = acc.astype(o_ref.dtype)


def _attn(q, k, v, bias, B, T, *, tq):
    M, D = q.shape
    H = D // HEAD_DIM
    nq = T // tq
    kv_spec = pl.BlockSpec((None, T, HEAD_DIM), lambda b, h, i: (b, 0, h))
    return pl.pallas_call(
        functools.partial(_attn_kernel, tq=tq, scale=HEAD_DIM ** -0.5),
        grid=(B, H, nq),
        in_specs=[
            pl.BlockSpec(memory_space=pltpu.SMEM),
            pl.BlockSpec((tq, HEAD_DIM), lambda b, h, i: (b * nq + i, h)),
            kv_spec, kv_spec,
        ],
        out_specs=pl.BlockSpec((tq, HEAD_DIM), lambda b, h, i: (b * nq + i, h)),
        out_shape=jax.ShapeDtypeStruct((M, D), BF16),
        compiler_params=_params(("arbitrary", "arbitrary", "arbitrary"), 32),
        name="attn",
    )(bias, q, k.reshape(B, T, D), v.reshape(B, T, D))


def _attn_paged_kernel(pt_ref, qbd_ref, bias_ref, kn_ref, vn_ref, *rest, n_pg, page, n_new,
                       n_qh, scale):
    k_refs, v_refs = rest[:n_pg], rest[n_pg:2 * n_pg]
    o_ref, acc_ref, run_ref = rest[2 * n_pg:]
    s = pl.program_id(1)
    qbd = qbd_ref[...]
    bias = bias_ref[...]
    lanes = qbd.shape[1]

    def tile(kb, vb, nk, masked):
        row = lax.broadcasted_iota(jnp.int32, (nk, nk), 0)
        col = lax.broadcasted_iota(jnp.int32, (nk, nk), 1)
        tri = jnp.where(col > row, 1.0, 0.0).astype(BF16)
        z = _dot(kb, qbd) * scale + bias
        lk, lb = _sb_terms(z)
        if masked:
            key = lax.broadcasted_iota(jnp.int32, (nk, lanes), 0)
            qry = lax.broadcasted_iota(jnp.int32, (nk, lanes), 1) % n_new
            mask = key < qry
            lk = jnp.where(mask, lk, 0.0)
        hi, lo = _split_bf16(lk)
        after = _dot(tri, hi) + _dot(tri, lo) + run_ref[...]
        a = jnp.exp(lb + after)
        if masked:
            a = jnp.where(mask, a, 0.0)
        acc_ref[...] += _dot(a.T[:n_qh].astype(BF16), vb)
        run_ref[...] += jnp.sum(lk, axis=0, keepdims=True)

    @pl.when(s == 0)
    def _():
        acc_ref[...] = jnp.zeros_like(acc_ref)
        run_ref[...] = jnp.zeros_like(run_ref)
        tile(kn_ref[...], vn_ref[...], kn_ref.shape[0], True)

    for p in range(n_pg):
        tile(k_refs[p][...].astype(BF16), v_refs[p][...].astype(BF16), page, False)

    @pl.when(s == pl.num_programs(1) - 1)
    def _():
        for hd in range(n_qh // n_new):
            sl = slice(hd * HEAD_DIM, (hd + 1) * HEAD_DIM)
            o_ref[:, sl] = acc_ref[hd * n_new:(hd + 1) * n_new, sl].astype(o_ref.dtype)


def _attn_paged(q, k_new, v_new, cache_k, cache_v, page_table, bias, B, T, *, n_pg):
    M, D = q.shape
    H = D // HEAD_DIM
    n_phys, page = cache_k.shape[:2]
    n_pages = page_table.shape[1]
    n_qh = H * T
    lanes = V7X_LANES * pl.cdiv(n_qh, V7X_LANES)
    new_rows = page
    q4 = q.reshape(B, T, H, HEAD_DIM)
    qbd = jnp.einsum('bthd,hg->bhdgt', q4, jnp.eye(H, dtype=q.dtype)).reshape(B, D, n_qh)
    qbd = jnp.pad(qbd, ((0, 0), (0, 0), (0, lanes - n_qh)))
    bias_row = jnp.pad(jnp.repeat(bias.astype(F32), T), (0, lanes - n_qh)).reshape(1, lanes)
    pad_new = lambda a: jnp.pad(a.reshape(B, T, D), ((0, 0), (0, new_rows - T), (0, 0)))
    ck = cache_k.reshape(n_phys, page, D)
    cv = cache_v.reshape(n_phys, page, D)

    def page_spec(p):
        return pl.BlockSpec((None, page, D),
                            lambda b, s, pt: (pt[b, n_pages - 1 - (s * n_pg + p)], 0, 0))

    grid_spec = pltpu.PrefetchScalarGridSpec(
        num_scalar_prefetch=1,
        grid=(B, n_pages // n_pg),
        in_specs=[
            pl.BlockSpec((None, D, lanes), lambda b, s, pt: (b, 0, 0)),
            pl.BlockSpec((1, lanes), lambda b, s, pt: (0, 0)),
            pl.BlockSpec((None, new_rows, D), lambda b, s, pt: (b, 0, 0)),
            pl.BlockSpec((None, new_rows, D), lambda b, s, pt: (b, 0, 0)),
        ] + [page_spec(p) for p in range(n_pg)] * 2,
        out_specs=pl.BlockSpec((None, T, D), lambda b, s, pt: (b, 0, 0)),
        scratch_shapes=[pltpu.VMEM((n_qh, D), F32), pltpu.VMEM((1, lanes), F32)],
    )
    o = pl.pallas_call(
        functools.partial(_attn_paged_kernel, n_pg=n_pg, page=page, n_new=T, n_qh=n_qh,
                          scale=HEAD_DIM ** -0.5),
        grid_spec=grid_spec,
        out_shape=jax.ShapeDtypeStruct((B, T, D), F32),
        compiler_params=_params(("arbitrary", "arbitrary"), 48),
        name="attn_paged",
    )(page_table, qbd, bias_row, pad_new(k_new), pad_new(v_new),
      *([ck] * n_pg), *([cv] * n_pg))
    return o.reshape(M, D).astype(BF16)


def _trunk(x3, mods_b, mods_kv_b, pos0, prev, past, wts, *, tm, tf, tn, tt, ffn_vmem):
    (norm_g, w_gate, w_up, w_down, w_pool, pool_scale, kv_norm_g, w_kv, k_norm_g, w_q, q_norm_g,
     sb_bias, w_o) = wts
    B, T, D = x3.shape
    M = B * T
    depth = norm_g.shape[0]
    n_a = w_pool.shape[0]
    per_row = T < tm
    if per_row:
        expand = lambda m: jnp.repeat(m.reshape(B, -1), T, axis=0)
    else:
        expand = lambda m: m
    tile = dict(tm=tm, rows_per_batch=T)
    x = x3.reshape(M, D)
    gain = lambda a: a.reshape(1, D)
    states = []
    k16 = v16 = k32 = v32 = None
    for l in range(depth):
        mods = expand(mods_b[l])
        x = _ffn(x, mods, (0, 1, 2), gain(norm_g[l, 0]), w_gate, w_up, w_down, l, 0,
                 tf=tf, vmem_mib=ffn_vmem, **tile)
        if l < n_a:
            y3, st = _pool(x.reshape(B, T, D), mods_b[l], (3, 4, 5), gain(norm_g[l, 1]), prev[l],
                           w_pool[l], pool_scale[l], pos0, tt=tt)
            x = y3.reshape(M, D)
            states.append(st[:, POOL_HALO - POOL_STATE:])
        else:
            i = l - n_a
            (q,) = _proj(x, mods, (3, 4), gain(norm_g[l, 1]), w_q, i, 0, D, q_norm_g[i], (BF16,),
                         tn=tn, vmem_mib=48, **tile)
            if past is None:
                o = _attn(q, k16, v16, sb_bias[i], B, T, tq=256)
            else:
                o = _attn_paged(q, k16, v16, past[0], past[1], past[2], sb_bias[i], B, T, n_pg=4)
            x = _oproj(o, x, mods, 5, w_o, i, tn=tn, vmem_mib=48, **tile)
        x = _ffn(x, mods, (6, 7, 8), gain(norm_g[l, 2]), w_gate, w_up, w_down, l, 1,
                 tf=tf, vmem_mib=ffn_vmem, **tile)
        if l == n_a - 1:
            mkv = expand(mods_kv_b)
            k32, k16 = _proj(x, mkv, (0, 1), gain(kv_norm_g), w_kv[None], 0, 0, D, k_norm_g,
                             (F32, BF16), tn=tn, vmem_mib=48, **tile)
            v32, v16 = _proj(x, mkv, (0, 1), gain(kv_norm_g), w_kv[None], 0, D, D, None,
                             (F32, BF16), tn=tn, vmem_mib=48, **tile)
    H = D // HEAD_DIM
    return (x.reshape(B, T, D), k32.reshape(B, T, H, HEAD_DIM), v32.reshape(B, T, H, HEAD_DIM),
            jnp.stack(states))


def kernel(x_prompt, x_sample, c_prompt, c_sample, cache_k, cache_v, state_pool, page_table, norm_g, w_ada, b_ada, w_gate, w_up, w_down, w_pool, pool_scale, kv_norm_g, w_ada_kv, b_ada_kv, w_kv, k_norm_g, w_q, q_norm_g, sb_bias, w_o):
    Bp, Tp, D = x_prompt.shape
    Bs, Ts, _ = x_sample.shape
    n_a = w_pool.shape[0]
    past_len = page_table.shape[1] * cache_k.shape[1]

    c_all = jnp.concatenate([c_prompt, c_sample], axis=0)
    rows = V7X_SUBLANES * 2 * pl.cdiv(Bp + Bs, V7X_SUBLANES * 2)
    c_all = jnp.pad(c_all, ((0, rows - (Bp + Bs)), (0, 0)))
    mods = _ada(c_all, w_ada, b_ada, tn=1024)
    mods_kv = _ada(c_all, w_ada_kv[None], b_ada_kv[None], tn=1024)[0]
    split = lambda m, lo, n: m[..., lo:lo + n, :][..., None, :]

    wts = (norm_g, w_gate, w_up, w_down, w_pool, pool_scale, kv_norm_g, w_kv, k_norm_g, w_q,
           q_norm_g, sb_bias, w_o)

    zero_prev = jnp.zeros((n_a, Bp, POOL_HALO, D), x_prompt.dtype)
    y_p, k_p, v_p, pool_p = _trunk(
        x_prompt, split(mods, 0, Bp), split(mods_kv, 0, Bp), 0, zero_prev, None, wts,
        tm=1024, tf=256, tn=512, tt=512, ffn_vmem=56)

    prev_s = jnp.pad(state_pool, ((0, 0), (0, 0), (POOL_HALO - POOL_STATE, 0), (0, 0)))
    y_s, k_s, v_s, pool_s = _trunk(
        x_sample, split(mods, Bp, Bs), split(mods_kv, Bp, Bs), past_len, prev_s,
        (cache_k, cache_v, page_table), wts,
        tm=Bs * Ts, tf=512, tn=512, tt=Ts, ffn_vmem=48)

    return (y_p, y_s, k_p, v_p, k_s, v_s, pool_p, pool_s)
```

```python
import functools

import jax
import jax.numpy as jnp
from jax import lax
from jax.experimental import pallas as pl
from jax.experimental.pallas import tpu as pltpu

V7X_LANES = 128
V7X_SUBLANES = 8
V7X_VMEM_BYTES = 64 * 1024 * 1024

POOL_WINDOWS = (2, 4, 8, 16)
POOL_STATE = max(POOL_WINDOWS) - 1
POOL_HALO = POOL_STATE + 1
HEAD_DIM = 128
N_MOD = 9
EPS = 1e-6

BF16 = jnp.bfloat16
F32 = jnp.float32


def _dot(a, b):
    return jnp.dot(a, b, preferred_element_type=F32)


def _dot_nt(a, b):
    return lax.dot_general(a, b, (((1,), (1,)), ((), ())), preferred_element_type=F32)


def _norm_mod(x, g, shift, scale):
    ms = jnp.mean(x * x, axis=-1, keepdims=True)
    return x * lax.rsqrt(ms + EPS) * (g * (1.0 + scale)) + shift


def _params(sem, vmem_mib):
    return pltpu.CompilerParams(dimension_semantics=sem,
                                vmem_limit_bytes=vmem_mib * 1024 * 1024)


def _mod_spec(mods, j, tm, rows_per_batch, width, ncol=1):
    if ncol == 1:
        col = lambda g: j
    else:
        col = lambda g: j * ncol + g[1]
    if mods.ndim == 3:
        tiles_per_batch = rows_per_batch // tm
        return pl.BlockSpec((None, 1, width), lambda *g: (g[0] // tiles_per_batch, 0, col(g)))
    return pl.BlockSpec((tm, width), lambda *g: (g[0], col(g)))


def _ada_kernel(c_ref, w_ref, b_ref, o_ref):
    c = c_ref[...]
    sc = (c * jax.nn.sigmoid(c)).astype(BF16)
    o_ref[...] = _dot(sc, w_ref[...].astype(BF16)) + b_ref[...]


def _ada(c, w, b, tn):
    L, D, N = w.shape
    R = c.shape[0]
    return pl.pallas_call(
        _ada_kernel,
        grid=(L, N // tn),
        in_specs=[
            pl.BlockSpec((R, D), lambda l, n: (0, 0)),
            pl.BlockSpec((None, D, tn), lambda l, n: (l, 0, n)),
            pl.BlockSpec((None, 1, tn), lambda l, n: (l, 0, n)),
        ],
        out_specs=pl.BlockSpec((None, R, tn), lambda l, n: (l, 0, n)),
        out_shape=jax.ShapeDtypeStruct((L, R, N), F32),
        compiler_params=_params(("arbitrary", "arbitrary"), 40),
        name="ada",
    )(c, w, b.reshape(L, 1, N))


def _ffn_kernel(x_ref, g_ref, sh_ref, sc_ref, gt_ref, wg_ref, wu_ref, wd_ref, o_ref, h_ref,
                *, nf, tf, last):
    f = pl.program_id(1)

    @pl.when(f == 0)
    def _():
        h_ref[...] = _norm_mod(x_ref[...], g_ref[...], sh_ref[...], sc_ref[...]).astype(BF16)
        o_ref[...] = jnp.zeros_like(o_ref)

    def step(width):
        h = h_ref[...]
        gate = _dot(h, wg_ref[:, :width].astype(BF16))
        up = _dot(h, wu_ref[:, :width].astype(BF16))
        a = (gate * jax.nn.sigmoid(gate) * up).astype(BF16)
        o_ref[...] += _dot(a, wd_ref[:width, :].astype(BF16))

    if last == tf:
        step(tf)
    else:
        pl.when(f < nf - 1)(lambda: step(tf))
        pl.when(f == nf - 1)(lambda: step(last))

    @pl.when(f == nf - 1)
    def _():
        o_ref[...] = x_ref[...] + 0.5 * gt_ref[...] * o_ref[...]


def _ffn(x, mods, midx, g, w_gate, w_up, w_down, l, j, *, tm, tf, rows_per_batch, vmem_mib):
    M, D = x.shape
    F = w_gate.shape[-1]
    nf = pl.cdiv(F, tf)
    last = F - (nf - 1) * tf
    ms = functools.partial(_mod_spec, mods, tm=tm, rows_per_batch=rows_per_batch, width=D)
    return pl.pallas_call(
        functools.partial(_ffn_kernel, nf=nf, tf=tf, last=last),
        grid=(M // tm, nf),
        in_specs=[
            pl.BlockSpec((tm, D), lambda i, f: (i, 0)),
            pl.BlockSpec((1, D), lambda i, f: (0, 0)),
            ms(j=midx[0]), ms(j=midx[1]), ms(j=midx[2]),
            pl.BlockSpec((None, None, D, tf), lambda i, f: (l, j, 0, f)),
            pl.BlockSpec((None, None, D, tf), lambda i, f: (l, j, 0, f)),
            pl.BlockSpec((None, None, tf, D), lambda i, f: (l, j, f, 0)),
        ],
        out_specs=pl.BlockSpec((tm, D), lambda i, f: (i, 0)),
        out_shape=jax.ShapeDtypeStruct((M, D), F32),
        scratch_shapes=[pltpu.VMEM((tm, D), BF16)],
        compiler_params=_params(("arbitrary", "arbitrary"), vmem_mib),
        name="ffn",
    )(x, g, mods, mods, mods, w_gate, w_up, w_down)


def _proj_kernel(x_ref, g_ref, sh_ref, sc_ref, w_ref, hg_ref, *rest, head_norm, n_out):
    outs, h_ref = rest[:n_out], rest[n_out]
    n = pl.program_id(1)

    @pl.when(n == 0)
    def _():
        h_ref[...] = _norm_mod(x_ref[...], g_ref[...], sh_ref[...], sc_ref[...]).astype(BF16)

    y = _dot(h_ref[...], w_ref[...].astype(BF16))
    if head_norm:
        hg = hg_ref[...]
        for c in range(y.shape[1] // HEAD_DIM):
            sl = slice(c * HEAD_DIM, (c + 1) * HEAD_DIM)
            yh = y[:, sl]
            ms = jnp.mean(yh * yh, axis=-1, keepdims=True)
            yn = yh * lax.rsqrt(ms + EPS) * hg
            for o in outs:
                o[:, sl] = yn.astype(o.dtype)
    else:
        for o in outs:
            o[...] = y.astype(o.dtype)


def _proj(x, mods, midx, g, w, wi, col0, n_cols, head_g, out_dtypes, *, tm, tn, rows_per_batch,
          vmem_mib):
    M, D = x.shape
    ms = functools.partial(_mod_spec, mods, tm=tm, rows_per_batch=rows_per_batch, width=D)
    cb = col0 // tn
    head_norm = head_g is not None
    if head_g is None:
        head_g = jnp.ones((HEAD_DIM,), F32)
    outs = pl.pallas_call(
        functools.partial(_proj_kernel, head_norm=head_norm, n_out=len(out_dtypes)),
        grid=(M // tm, n_cols // tn),
        in_specs=[
            pl.BlockSpec((tm, D), lambda i, n: (i, 0)),
            pl.BlockSpec((1, D), lambda i, n: (0, 0)),
            ms(j=midx[0]), ms(j=midx[1]),
            pl.BlockSpec((None, D, tn), lambda i, n: (wi, 0, cb + n)),
            pl.BlockSpec((1, HEAD_DIM), lambda i, n: (0, 0)),
        ],
        out_specs=[pl.BlockSpec((tm, tn), lambda i, n: (i, n)) for _ in out_dtypes],
        out_shape=[jax.ShapeDtypeStruct((M, n_cols), dt) for dt in out_dtypes],
        scratch_shapes=[pltpu.VMEM((tm, D), BF16)],
        compiler_params=_params(("arbitrary", "arbitrary"), vmem_mib),
        name="proj",
    )(x, g, mods, mods, w, head_g.reshape(1, HEAD_DIM))
    return outs


def _oproj_kernel(o_ref, w_ref, x_ref, gt_ref, y_ref):
    y_ref[...] = x_ref[...] + gt_ref[...] * _dot(o_ref[...], w_ref[...].astype(BF16))


def _oproj(o, x, mods, gidx, w, wi, *, tm, tn, rows_per_batch, vmem_mib):
    M, D = x.shape
    return pl.pallas_call(
        _oproj_kernel,
        grid=(M // tm, D // tn),
        in_specs=[
            pl.BlockSpec((tm, D), lambda i, n: (i, 0)),
            pl.BlockSpec((None, D, tn), lambda i, n: (wi, 0, n)),
            pl.BlockSpec((tm, tn), lambda i, n: (i, n)),
            _mod_spec(mods, gidx, tm, rows_per_batch, tn, ncol=D // tn),
        ],
        out_specs=pl.BlockSpec((tm, tn), lambda i, n: (i, n)),
        out_shape=jax.ShapeDtypeStruct((M, D), F32),
        compiler_params=_params(("arbitrary", "arbitrary"), vmem_mib),
        name="oproj",
    )(o, w, x, mods)


def _pool_kernel(x_ref, g_ref, sh_ref, sc_ref, gt_ref, prev_ref, w_ref, ps_ref, y_ref, st_ref,
                 buf_ref, *, tt, pos0):
    t = pl.program_id(1)
    D = x_ref.shape[-1]
    group = D // len(POOL_WINDOWS)

    @pl.when(t == 0)
    def _():
        buf_ref[0:POOL_HALO, :] = prev_ref[...]

    @pl.when(t > 0)
    def _():
        buf_ref[0:POOL_HALO, :] = buf_ref[tt:tt + POOL_HALO, :]

    x = x_ref[...]
    buf_ref[POOL_HALO:POOL_HALO + tt, :] = _norm_mod(x, g_ref[...], sh_ref[...], sc_ref[...])
    st_ref[...] = buf_ref[tt:tt + POOL_HALO, :]

    pos = (pos0 + t * tt + lax.broadcasted_iota(jnp.int32, (tt, 1), 0)).astype(F32)
    for gi, w in enumerate(POOL_WINDOWS):
        sl = slice(gi * group, (gi + 1) * group)
        h = buf_ref[POOL_HALO:POOL_HALO + tt, sl]
        wsum = h
        for back in range(1, w):
            wsum = wsum + buf_ref[POOL_HALO - back:POOL_HALO - back + tt, sl]
        cnt = jnp.minimum(float(w), pos + 1.0)
        diff = (wsum / cnt - h).astype(BF16)
        m = _dot(diff, w_ref[gi].astype(BF16)) * ps_ref[:, sl]
        y_ref[:, sl] = x[:, sl] + gt_ref[:, sl] * m


def _pool(x3, mods_b, midx, g, prev, w_pool_l, pool_scale_l, pos0, *, tt):
    B, T, D = x3.shape
    ms = lambda j: pl.BlockSpec((None, 1, D), lambda b, t: (b, 0, j))
    return pl.pallas_call(
        functools.partial(_pool_kernel, tt=tt, pos0=pos0),
        grid=(B, T // tt),
        in_specs=[
            pl.BlockSpec((None, tt, D), lambda b, t: (b, t, 0)),
            pl.BlockSpec((1, D), lambda b, t: (0, 0)),
            ms(midx[0]), ms(midx[1]), ms(midx[2]),
            pl.BlockSpec((None, POOL_HALO, D), lambda b, t: (b, 0, 0)),
            pl.BlockSpec(w_pool_l.shape, lambda b, t: (0, 0, 0)),
            pl.BlockSpec((1, D), lambda b, t: (0, 0)),
        ],
        out_specs=[
            pl.BlockSpec((None, tt, D), lambda b, t: (b, t, 0)),
            pl.BlockSpec((None, POOL_HALO, D), lambda b, t: (b, 0, 0)),
        ],
        out_shape=[jax.ShapeDtypeStruct((B, T, D), F32),
                   jax.ShapeDtypeStruct((B, POOL_HALO, D), F32)],
        scratch_shapes=[pltpu.VMEM((POOL_HALO + tt, D), F32)],
        compiler_params=_params(("arbitrary", "arbitrary"), 48),
        name="pool",
    )(x3, g, mods_b, mods_b, mods_b, prev, w_pool_l, pool_scale_l.reshape(1, D))


def _softplus(z):
    neg_abs = lax.bitcast_convert_type(
        lax.bitcast_convert_type(z, jnp.uint32) | jnp.uint32(0x80000000), F32)
    return jnp.maximum(z, 0.0) + jnp.log(1.0 + jnp.exp(neg_abs))


def _split_bf16(x, axis):
    hi = x.astype(BF16)
    lo = (x - hi.astype(F32)).astype(BF16)
    return jnp.concatenate([hi, lo], axis=axis)


def _attn_kernel(bias_ref, q_ref, k_ref, v_ref, o_ref, *, tq, hps, scale):
    hg = pl.program_id(1)
    i = pl.program_id(2)
    row = lax.broadcasted_iota(jnp.int32, (tq, tq), 0)
    col = lax.broadcasted_iota(jnp.int32, (tq, tq), 1)
    tri = jnp.where(row >= col, 1.0, 0.0).astype(BF16)
    tri2 = jnp.concatenate([tri, tri], axis=0)
    causal = col < row
    heads = [slice(c * HEAD_DIM, (c + 1) * HEAD_DIM) for c in range(hps)]
    bias = [bias_ref[hg * hps + c] for c in range(hps)]

    def tiles(k0, totals, mask):
        zs = [_dot_nt(q_ref[:, hd], k_ref[pl.ds(k0, tq), hd]) * scale + b
              for hd, b in zip(heads, bias)]
        sps = [_softplus(z) for z in zs]
        if mask is not None:
            sps = [jnp.where(mask, sp, 0.0) for sp in sps]
        sums = [_dot(_split_bf16(sp, 1), tri2) for sp in sps]
        probs = [jnp.exp(z - sm) for z, sm in zip(zs, sums)]
        if mask is not None:
            probs = [jnp.where(mask, a, 0.0) for a in probs]
        pvs = [_dot(a.astype(BF16), v_ref[pl.ds(k0, tq), hd]) * jnp.exp(-total)
               for a, hd, total in zip(probs, heads, totals)]
        totals = [total + sm[:, :1] for total, sm in zip(totals, sums)]
        return pvs, totals

    d0 = pl.multiple_of(i * tq, tq)
    accs, totals = tiles(d0, [jnp.zeros((tq, 1), F32)] * hps, causal)

    def body(n, carry):
        accs, totals = carry
        k0 = pl.multiple_of((i - 1 - n) * tq, tq)
        pvs, totals = tiles(k0, totals, None)
        return [acc + pv for acc, pv in zip(accs, pvs)], totals

    accs, totals = lax.fori_loop(0, i, body, (accs, totals))
    for acc, hd in zip(accs, heads):
        o_ref[:, hd] = acc.astype(o_ref.dtype)


def _attn(q, k, v, bias, B, T, *, tq, hps):
    M, D = q.shape
    H = D // HEAD_DIM
    nq = T // tq
    w = hps * HEAD_DIM
    kv_spec = pl.BlockSpec((None, T, w), lambda b, h, i: (b, 0, h))
    return pl.pallas_call(
        functools.partial(_attn_kernel, tq=tq, hps=hps, scale=HEAD_DIM ** -0.5),
        grid=(B, H // hps, nq),
        in_specs=[
            pl.BlockSpec(memory_space=pltpu.SMEM),
            pl.BlockSpec((tq, w), lambda b, h, i: (b * nq + i, h)),
            kv_spec, kv_spec,
        ],
        out_specs=pl.BlockSpec((tq, w), lambda b, h, i: (b * nq + i, h)),
        out_shape=jax.ShapeDtypeStruct((M, D), BF16),
        compiler_params=_params(("arbitrary", "arbitrary", "arbitrary"), 32),
        name="attn",
    )(bias, q, k.reshape(B, T, D), v.reshape(B, T, D))


MXU_BF16_ROWS = 16


def _attn_paged_kernel(pt_ref, qbd_ref, bias_ref, kn_ref, vn_ref, *rest, n_pg, page, n_new,
                       n_heads, scale):
    k_refs, v_refs = rest[:n_pg], rest[n_pg:2 * n_pg]
    o_ref, acc_ref, tot_ref = rest[2 * n_pg:]
    s = pl.program_id(1)
    lanes = qbd_ref.shape[1]
    heads_per_group = MXU_BF16_ROWS // n_new
    row = lax.broadcasted_iota(jnp.int32, (page, page), 0)
    col = lax.broadcasted_iota(jnp.int32, (page, page), 1)
    tri = jnp.where(col >= row, 1.0, 0.0).astype(BF16)
    tri2 = jnp.concatenate([tri, tri], axis=1)

    head = lambda hd: slice(hd * HEAD_DIM, (hd + 1) * HEAD_DIM)
    paged = lambda ref: (lambda hd: ref[pl.ds(hd, page, stride=n_heads), :].astype(BF16))
    dense = lambda ref: (lambda hd: ref[:, head(hd)])

    def tiles(slabs, masked):
        zs = []
        for k_slab, _ in slabs:
            z = None
            for hd in range(n_heads):
                part = _dot(k_slab(hd), qbd_ref[head(hd), :])
                z = part if z is None else z + part
            zs.append(z * scale + bias_ref[...])
        sps = [_softplus(z) for z in zs]
        if masked:
            key = lax.broadcasted_iota(jnp.int32, (page, lanes), 0)
            qry = lax.broadcasted_iota(jnp.int32, (page, lanes), 1) % n_new
            mask = key < qry
            sps = [jnp.where(mask, sp, 0.0) for sp in sps]
        total = tot_ref[...]
        totals = []
        for sp in sps:
            totals.append(total)
            total = total + jnp.sum(sp, axis=0, keepdims=True)
        tot_ref[...] = total
        sums = [_dot(tri2, _split_bf16(sp, 0)) for sp in sps]
        probs = [jnp.exp(z - sm - t) for z, sm, t in zip(zs, sums, totals)]
        if masked:
            probs = [jnp.where(mask, a, 0.0) for a in probs]
        pts = [a.T.astype(BF16) for a in probs]
        for hd in range(n_heads):
            g0 = (hd // heads_per_group) * MXU_BF16_ROWS
            pv = None
            for pt, (_, v_slab) in zip(pts, slabs):
                part = _dot(pt[g0:g0 + MXU_BF16_ROWS], v_slab(hd))
                pv = part if pv is None else pv + part
            acc_ref[hd] += pv

    @pl.when(s == 0)
    def _():
        acc_ref[...] = jnp.zeros_like(acc_ref)
        tot_ref[...] = jnp.zeros_like(tot_ref)
        tiles([(dense(kn_ref), dense(vn_ref))], True)

    tiles([(paged(k_refs[p]), paged(v_refs[p])) for p in range(n_pg)], False)

    @pl.when(s == pl.num_programs(1) - 1)
    def _():
        for hd in range(n_heads):
            r0 = (hd % heads_per_group) * n_new
            o_ref[:, head(hd)] = acc_ref[hd, r0:r0 + n_new, :]


def _attn_paged(q, k_new, v_new, cache_k, cache_v, page_table, bias, B, T, *, n_pg):
    M, D = q.shape
    n_phys, page, H, _ = cache_k.shape
    n_pages = page_table.shape[1]
    assert MXU_BF16_ROWS % T == 0 and n_pages % n_pg == 0
    n_qh = H * T
    lanes = V7X_LANES * pl.cdiv(n_qh, V7X_LANES)
    q4 = q.reshape(B, T, H, HEAD_DIM)
    qbd = jnp.einsum('bthd,hg->bhdgt', q4, jnp.eye(H, dtype=q.dtype)).reshape(B, D, n_qh)
    qbd = jnp.pad(qbd, ((0, 0), (0, 0), (0, lanes - n_qh)))
    bias_row = jnp.pad(jnp.repeat(bias.astype(F32), T), (0, lanes - n_qh)).reshape(1, lanes)
    pad_new = lambda a: jnp.pad(a.reshape(B, T, D), ((0, 0), (0, page - T), (0, 0)))

    ck = cache_k.reshape(n_phys, page * H, HEAD_DIM)
    cv = cache_v.reshape(n_phys, page * H, HEAD_DIM)

    def page_spec(p):
        return pl.BlockSpec((None, page * H, HEAD_DIM),
                            lambda b, s, pt: (pt[b, n_pages - 1 - (s * n_pg + p)], 0, 0))

    grid_spec = pltpu.PrefetchScalarGridSpec(
        num_scalar_prefetch=1,
        grid=(B, n_pages // n_pg),
        in_specs=[
            pl.BlockSpec((None, D, lanes), lambda b, s, pt: (b, 0, 0)),
            pl.BlockSpec((1, lanes), lambda b, s, pt: (0, 0)),
            pl.BlockSpec((None, page, D), lambda b, s, pt: (b, 0, 0)),
            pl.BlockSpec((None, page, D), lambda b, s, pt: (b, 0, 0)),
        ] + [page_spec(p) for p in range(n_pg)] * 2,
        out_specs=pl.BlockSpec((None, T, D), lambda b, s, pt: (b, 0, 0)),
        scratch_shapes=[pltpu.VMEM((H, MXU_BF16_ROWS, HEAD_DIM), F32),
                        pltpu.VMEM((1, lanes), F32)],
    )
    o = pl.pallas_call(
        functools.partial(_attn_paged_kernel, n_pg=n_pg, page=page, n_new=T, n_heads=H,
                          scale=HEAD_DIM ** -0.5),
        grid_spec=grid_spec,
        out_shape=jax.ShapeDtypeStruct((B, T, D), F32),
        compiler_params=_params(("arbitrary", "arbitrary"), 48),
        name="attn_paged",
    )(page_table, qbd, bias_row, pad_new(k_new), pad_new(v_new),
      *([ck] * n_pg), *([cv] * n_pg))
    return o.reshape(M, D).astype(BF16)


def _trunk(x3, mods_b, mods_kv_b, pos0, prev, past, wts, *, tm, tf, tn, tt, ffn_vmem):
    (norm_g, w_gate, w_up, w_down, w_pool, pool_scale, kv_norm_g, w_kv, k_norm_g, w_q, q_norm_g,
     sb_bias, w_o) = wts
    B, T, D = x3.shape
    M = B * T
    depth = norm_g.shape[0]
    n_a = w_pool.shape[0]
    per_row = T < tm
    if per_row:
        expand = lambda m: jnp.repeat(m.reshape(B, -1), T, axis=0)
    else:
        expand = lambda m: m
    tile = dict(tm=tm, rows_per_batch=T)
    x = x3.reshape(M, D)
    gain = lambda a: a.reshape(1, D)
    states = []
    k16 = v16 = k32 = v32 = None
    for l in range(depth):
        mods = expand(mods_b[l])
        x = _ffn(x, mods, (0, 1, 2), gain(norm_g[l, 0]), w_gate, w_up, w_down, l, 0,
                 tf=tf, vmem_mib=ffn_vmem, **tile)
        if l < n_a:
            y3, st = _pool(x.reshape(B, T, D), mods_b[l], (3, 4, 5), gain(norm_g[l, 1]), prev[l],
                           w_pool[l], pool_scale[l], pos0, tt=tt)
            x = y3.reshape(M, D)
            states.append(st[:, POOL_HALO - POOL_STATE:])
        else:
            i = l - n_a
            (q,) = _proj(x, mods, (3, 4), gain(norm_g[l, 1]), w_q, i, 0, D, q_norm_g[i], (BF16,),
                         tn=tn, vmem_mib=48, **tile)
            if past is None:
                o = _attn(q, k16, v16, sb_bias[i], B, T, tq=256, hps=4)
            else:
                o = _attn_paged(q, k16, v16, past[0], past[1], past[2], sb_bias[i], B, T, n_pg=4)
            x = _oproj(o, x, mods, 5, w_o, i, tn=tn, vmem_mib=48, **tile)
        x = _ffn(x, mods, (6, 7, 8), gain(norm_g[l, 2]), w_gate, w_up, w_down, l, 1,
                 tf=tf, vmem_mib=ffn_vmem, **tile)
        if l == n_a - 1:
            mkv = expand(mods_kv_b)
            k32, k16 = _proj(x, mkv, (0, 1), gain(kv_norm_g), w_kv[None], 0, 0, D, k_norm_g,
                             (F32, BF16), tn=tn, vmem_mib=48, **tile)
            v32, v16 = _proj(x, mkv, (0, 1), gain(kv_norm_g), w_kv[None], 0, D, D, None,
                             (F32, BF16), tn=tn, vmem_mib=48, **tile)
    H = D // HEAD_DIM
    return (x.reshape(B, T, D), k32.reshape(B, T, H, HEAD_DIM), v32.reshape(B, T, H, HEAD_DIM),
            jnp.stack(states))


def kernel(x_prompt, x_sample, c_prompt, c_sample, cache_k, cache_v, state_pool, page_table, norm_g, w_ada, b_ada, w_gate, w_up, w_down, w_pool, pool_scale, kv_norm_g, w_ada_kv, b_ada_kv, w_kv, k_norm_g, w_q, q_norm_g, sb_bias, w_o):
    Bp, Tp, D = x_prompt.shape
    Bs, Ts, _ = x_sample.shape
    n_a = w_pool.shape[0]
    past_len = page_table.shape[1] * cache_k.shape[1]

    c_all = jnp.concatenate([c_prompt, c_sample], axis=0)
    rows = V7X_SUBLANES * 2 * pl.cdiv(Bp + Bs, V7X_SUBLANES * 2)
    c_all = jnp.pad(c_all, ((0, rows - (Bp + Bs)), (0, 0)))
    mods = _ada(c_all, w_ada, b_ada, tn=1024)
    mods_kv = _ada(c_all, w_ada_kv[None], b_ada_kv[None], tn=1024)[0]
    split = lambda m, lo, n: m[..., lo:lo + n, :][..., None, :]

    wts = (norm_g, w_gate, w_up, w_down, w_pool, pool_scale, kv_norm_g, w_kv, k_norm_g, w_q,
           q_norm_g, sb_bias, w_o)

    zero_prev = jnp.zeros((n_a, Bp, POOL_HALO, D), x_prompt.dtype)
    y_p, k_p, v_p, pool_p = _trunk(
        x_prompt, split(mods, 0, Bp), split(mods_kv, 0, Bp), 0, zero_prev, None, wts,
        tm=1024, tf=256, tn=512, tt=512, ffn_vmem=56)

    prev_s = jnp.pad(state_pool, ((0, 0), (0, 0), (POOL_HALO - POOL_STATE, 0), (0, 0)))
    y_s, k_s, v_s, pool_s = _trunk(
        x_sample, split(mods, Bp, Bs), split(mods_kv, Bp, Bs), past_len, prev_s,
        (cache_k, cache_v, page_table), wts,
        tm=Bs * Ts, tf=512, tn=512, tt=Ts, ffn_vmem=48)

    return (y_p, y_s, k_p, v_p, k_s, v_s, pool_p, pool_s)
```

```python
import functools

import jax
import jax.numpy as jnp
from jax import lax
from jax.experimental import pallas as pl
from jax.experimental.pallas import tpu as pltpu

V7X_LANES = 128
V7X_SUBLANES = 8
V7X_VMEM_BYTES = 64 * 1024 * 1024

POOL_WINDOWS = (2, 4, 8, 16)
POOL_STATE = max(POOL_WINDOWS) - 1
POOL_HALO = POOL_STATE + 1
HEAD_DIM = 128
N_MOD = 9
EPS = 1e-6

BF16 = jnp.bfloat16
F32 = jnp.float32


def _dot(a, b):
    return jnp.dot(a, b, preferred_element_type=F32)


def _dot_nt(a, b):
    return lax.dot_general(a, b, (((1,), (1,)), ((), ())), preferred_element_type=F32)


def _norm_mod(x, g, shift, scale):
    ms = jnp.mean(x * x, axis=-1, keepdims=True)
    return x * lax.rsqrt(ms + EPS) * (g * (1.0 + scale)) + shift


def _params(sem, vmem_mib):
    return pltpu.CompilerParams(dimension_semantics=sem,
                                vmem_limit_bytes=vmem_mib * 1024 * 1024)


def _mod_spec(mods, j, tm, rows_per_batch, width, ncol=1):
    if ncol == 1:
        col = lambda g: j
    else:
        col = lambda g: j * ncol + g[1]
    if mods.ndim == 3:
        tiles_per_batch = rows_per_batch // tm
        return pl.BlockSpec((None, 1, width), lambda *g: (g[0] // tiles_per_batch, 0, col(g)))
    return pl.BlockSpec((tm, width), lambda *g: (g[0], col(g)))


def _ada_kernel(c_ref, w_ref, b_ref, o_ref):
    c = c_ref[...]
    sc = (c * jax.nn.sigmoid(c)).astype(BF16)
    o_ref[...] = _dot(sc, w_ref[...].astype(BF16)) + b_ref[...]


def _ada(c, w, b, tn):
    L, D, N = w.shape
    R = c.shape[0]
    return pl.pallas_call(
        _ada_kernel,
        grid=(L, N // tn),
        in_specs=[
            pl.BlockSpec((R, D), lambda l, n: (0, 0)),
            pl.BlockSpec((None, D, tn), lambda l, n: (l, 0, n)),
            pl.BlockSpec((None, 1, tn), lambda l, n: (l, 0, n)),
        ],
        out_specs=pl.BlockSpec((None, R, tn), lambda l, n: (l, 0, n)),
        out_shape=jax.ShapeDtypeStruct((L, R, N), F32),
        compiler_params=_params(("arbitrary", "arbitrary"), 40),
        name="ada",
    )(c, w, b.reshape(L, 1, N))


def _ffn_kernel(x_ref, g_ref, sh_ref, sc_ref, gt_ref, wg_ref, wu_ref, wd_ref, o_ref, h_ref,
                *, nf, tf, last):
    f = pl.program_id(1)

    @pl.when(f == 0)
    def _():
        h_ref[...] = _norm_mod(x_ref[...], g_ref[...], sh_ref[...], sc_ref[...]).astype(BF16)
        o_ref[...] = jnp.zeros_like(o_ref)

    def step(width):
        h = h_ref[...]
        gate = _dot(h, wg_ref[:, :width].astype(BF16))
        up = _dot(h, wu_ref[:, :width].astype(BF16))
        a = (gate * jax.nn.sigmoid(gate) * up).astype(BF16)
        o_ref[...] += _dot(a, wd_ref[:width, :].astype(BF16))

    if last == tf:
        step(tf)
    else:
        pl.when(f < nf - 1)(lambda: step(tf))
        pl.when(f == nf - 1)(lambda: step(last))

    @pl.when(f == nf - 1)
    def _():
        o_ref[...] = x_ref[...] + 0.5 * gt_ref[...] * o_ref[...]


def _ffn(x, mods, midx, g, w_gate, w_up, w_down, l, j, *, tm, tf, rows_per_batch, vmem_mib):
    M, D = x.shape
    F = w_gate.shape[-1]
    nf = pl.cdiv(F, tf)
    last = F - (nf - 1) * tf
    ms = functools.partial(_mod_spec, mods, tm=tm, rows_per_batch=rows_per_batch, width=D)
    return pl.pallas_call(
        functools.partial(_ffn_kernel, nf=nf, tf=tf, last=last),
        grid=(M // tm, nf),
        in_specs=[
            pl.BlockSpec((tm, D), lambda i, f: (i, 0)),
            pl.BlockSpec((1, D), lambda i, f: (0, 0)),
            ms(j=midx[0]), ms(j=midx[1]), ms(j=midx[2]),
            pl.BlockSpec((None, None, D, tf), lambda i, f: (l, j, 0, f)),
            pl.BlockSpec((None, None, D, tf), lambda i, f: (l, j, 0, f)),
            pl.BlockSpec((None, None, tf, D), lambda i, f: (l, j, f, 0)),
        ],
        out_specs=pl.BlockSpec((tm, D), lambda i, f: (i, 0)),
        out_shape=jax.ShapeDtypeStruct((M, D), F32),
        scratch_shapes=[pltpu.VMEM((tm, D), BF16)],
        compiler_params=_params(("arbitrary", "arbitrary"), vmem_mib),
        name="ffn",
    )(x, g, mods, mods, mods, w_gate, w_up, w_down)


def _proj_kernel(x_ref, g_ref, sh_ref, sc_ref, w_ref, hg_ref, *rest, head_norm, n_out):
    outs, h_ref = rest[:n_out], rest[n_out]
    n = pl.program_id(1)

    @pl.when(n == 0)
    def _():
        h_ref[...] = _norm_mod(x_ref[...], g_ref[...], sh_ref[...], sc_ref[...]).astype(BF16)

    y = _dot(h_ref[...], w_ref[...].astype(BF16))
    if head_norm:
        hg = hg_ref[...]
        for c in range(y.shape[1] // HEAD_DIM):
            sl = slice(c * HEAD_DIM, (c + 1) * HEAD_DIM)
            yh = y[:, sl]
            ms = jnp.mean(yh * yh, axis=-1, keepdims=True)
            yn = yh * lax.rsqrt(ms + EPS) * hg
            for o in outs:
                o[:, sl] = yn.astype(o.dtype)
    else:
        for o in outs:
            o[...] = y.astype(o.dtype)


def _proj(x, mods, midx, g, w, wi, col0, n_cols, head_g, out_dtypes, *, tm, tn, rows_per_batch,
          vmem_mib):
    M, D = x.shape
    ms = functools.partial(_mod_spec, mods, tm=tm, rows_per_batch=rows_per_batch, width=D)
    cb = col0 // tn
    head_norm = head_g is not None
    if head_g is None:
        head_g = jnp.ones((HEAD_DIM,), F32)
    outs = pl.pallas_call(
        functools.partial(_proj_kernel, head_norm=head_norm, n_out=len(out_dtypes)),
        grid=(M // tm, n_cols // tn),
        in_specs=[
            pl.BlockSpec((tm, D), lambda i, n: (i, 0)),
            pl.BlockSpec((1, D), lambda i, n: (0, 0)),
            ms(j=midx[0]), ms(j=midx[1]),
            pl.BlockSpec((None, D, tn), lambda i, n: (wi, 0, cb + n)),
            pl.BlockSpec((1, HEAD_DIM), lambda i, n: (0, 0)),
        ],
        out_specs=[pl.BlockSpec((tm, tn), lambda i, n: (i, n)) for _ in out_dtypes],
        out_shape=[jax.ShapeDtypeStruct((M, n_cols), dt) for dt in out_dtypes],
        scratch_shapes=[pltpu.VMEM((tm, D), BF16)],
        compiler_params=_params(("arbitrary", "arbitrary"), vmem_mib),
        name="proj",
    )(x, g, mods, mods, w, head_g.reshape(1, HEAD_DIM))
    return outs


def _oproj_kernel(o_ref, w_ref, x_ref, gt_ref, y_ref):
    y_ref[...] = x_ref[...] + gt_ref[...] * _dot(o_ref[...], w_ref[...].astype(BF16))


def _oproj(o, x, mods, gidx, w, wi, *, tm, tn, rows_per_batch, vmem_mib):
    M, D = x.shape
    return pl.pallas_call(
        _oproj_kernel,
        grid=(M // tm, D // tn),
        in_specs=[
            pl.BlockSpec((tm, D), lambda i, n: (i, 0)),
            pl.BlockSpec((None, D, tn), lambda i, n: (wi, 0, n)),
            pl.BlockSpec((tm, tn), lambda i, n: (i, n)),
            _mod_spec(mods, gidx, tm, rows_per_batch, tn, ncol=D // tn),
        ],
        out_specs=pl.BlockSpec((tm, tn), lambda i, n: (i, n)),
        out_shape=jax.ShapeDtypeStruct((M, D), F32),
        compiler_params=_params(("arbitrary", "arbitrary"), vmem_mib),
        name="oproj",
    )(o, w, x, mods)


def _pool_kernel(x_ref, g_ref, sh_ref, sc_ref, gt_ref, prev_ref, w_ref, ps_ref, y_ref, st_ref,
                 buf_ref, *, tt, pos0):
    t = pl.program_id(1)
    D = x_ref.shape[-1]
    group = D // len(POOL_WINDOWS)

    @pl.when(t == 0)
    def _():
        buf_ref[0:POOL_HALO, :] = prev_ref[...]

    @pl.when(t > 0)
    def _():
        buf_ref[0:POOL_HALO, :] = buf_ref[tt:tt + POOL_HALO, :]

    x = x_ref[...]
    buf_ref[POOL_HALO:POOL_HALO + tt, :] = _norm_mod(x, g_ref[...], sh_ref[...], sc_ref[...])
    st_ref[...] = buf_ref[tt:tt + POOL_HALO, :]

    pos = (pos0 + t * tt + lax.broadcasted_iota(jnp.int32, (tt, 1), 0)).astype(F32)
    for gi, w in enumerate(POOL_WINDOWS):
        sl = slice(gi * group, (gi + 1) * group)
        h = buf_ref[POOL_HALO:POOL_HALO + tt, sl]
        wsum = h
        for back in range(1, w):
            wsum = wsum + buf_ref[POOL_HALO - back:POOL_HALO - back + tt, sl]
        cnt = jnp.minimum(float(w), pos + 1.0)
        diff = (wsum / cnt - h).astype(BF16)
        m = _dot(diff, w_ref[gi].astype(BF16)) * ps_ref[:, sl]
        y_ref[:, sl] = x[:, sl] + gt_ref[:, sl] * m


def _pool(x3, mods_b, midx, g, prev, w_pool_l, pool_scale_l, pos0, *, tt):
    B, T, D = x3.shape
    ms = lambda j: pl.BlockSpec((None, 1, D), lambda b, t: (b, 0, j))
    return pl.pallas_call(
        functools.partial(_pool_kernel, tt=tt, pos0=pos0),
        grid=(B, T // tt),
        in_specs=[
            pl.BlockSpec((None, tt, D), lambda b, t: (b, t, 0)),
            pl.BlockSpec((1, D), lambda b, t: (0, 0)),
            ms(midx[0]), ms(midx[1]), ms(midx[2]),
            pl.BlockSpec((None, POOL_HALO, D), lambda b, t: (b, 0, 0)),
            pl.BlockSpec(w_pool_l.shape, lambda b, t: (0, 0, 0)),
            pl.BlockSpec((1, D), lambda b, t: (0, 0)),
        ],
        out_specs=[
            pl.BlockSpec((None, tt, D), lambda b, t: (b, t, 0)),
            pl.BlockSpec((None, POOL_HALO, D), lambda b, t: (b, 0, 0)),
        ],
        out_shape=[jax.ShapeDtypeStruct((B, T, D), F32),
                   jax.ShapeDtypeStruct((B, POOL_HALO, D), F32)],
        scratch_shapes=[pltpu.VMEM((POOL_HALO + tt, D), F32)],
        compiler_params=_params(("arbitrary", "arbitrary"), 48),
        name="pool",
    )(x3, g, mods_b, mods_b, mods_b, prev, w_pool_l, pool_scale_l.reshape(1, D))


LOG2E = 1.4426950408889634


def _softplus2(y):
    neg_abs = lax.bitcast_convert_type(
        lax.bitcast_convert_type(y, jnp.uint32) | jnp.uint32(0x80000000), F32)
    return jnp.maximum(y, 0.0) + jnp.log2(1.0 + jnp.exp2(neg_abs))


def _split_bf16(x, axis):
    hi = x.astype(BF16)
    lo = (x - hi.astype(F32)).astype(BF16)
    return jnp.concatenate([hi, lo], axis=axis)


def _attn_kernel(bias_ref, q_ref, k_ref, v_ref, o_ref, *, tq, hps, scale):
    hg = pl.program_id(1)
    i = pl.program_id(2)
    row = lax.broadcasted_iota(jnp.int32, (tq, tq), 0)
    col = lax.broadcasted_iota(jnp.int32, (tq, tq), 1)
    tri = jnp.where(row >= col, 1.0, 0.0).astype(BF16)
    tri2 = jnp.concatenate([tri, tri], axis=0)
    causal = col < row
    heads = [slice(c * HEAD_DIM, (c + 1) * HEAD_DIM) for c in range(hps)]
    bias = [bias_ref[hg * hps + c] * LOG2E for c in range(hps)]

    def tiles(k0, totals, mask):
        zs = [_dot_nt(q_ref[:, hd], k_ref[pl.ds(k0, tq), hd]) * (scale * LOG2E) + b
              for hd, b in zip(heads, bias)]
        sps = [_softplus2(z) for z in zs]
        if mask is not None:
            sps = [jnp.where(mask, sp, 0.0) for sp in sps]
        sums = [_dot(_split_bf16(sp, 1), tri2) for sp in sps]
        probs = [jnp.exp2(z - sm) for z, sm in zip(zs, sums)]
        if mask is not None:
            probs = [jnp.where(mask, a, 0.0) for a in probs]
        pvs = [_dot(a.astype(BF16), v_ref[pl.ds(k0, tq), hd]) * jnp.exp2(-total)
               for a, hd, total in zip(probs, heads, totals)]
        totals = [total + sm[:, :1] for total, sm in zip(totals, sums)]
        return pvs, totals

    d0 = pl.multiple_of(i * tq, tq)
    accs, totals = tiles(d0, [jnp.zeros((tq, 1), F32)] * hps, causal)

    def body(n, carry):
        accs, totals = carry
        k0 = pl.multiple_of((i - 1 - n) * tq, tq)
        pvs, totals = tiles(k0, totals, None)
        return [acc + pv for acc, pv in zip(accs, pvs)], totals

    accs, totals = lax.fori_loop(0, i, body, (accs, totals))
    for acc, hd in zip(accs, heads):
        o_ref[:, hd] = acc.astype(o_ref.dtype)


def _attn(q, k, v, bias, B, T, *, tq, hps):
    M, D = q.shape
    H = D // HEAD_DIM
    nq = T // tq
    w = hps * HEAD_DIM
    kv_spec = pl.BlockSpec((None, T, w), lambda b, h, i: (b, 0, h))
    return pl.pallas_call(
        functools.partial(_attn_kernel, tq=tq, hps=hps, scale=HEAD_DIM ** -0.5),
        grid=(B, H // hps, nq),
        in_specs=[
            pl.BlockSpec(memory_space=pltpu.SMEM),
            pl.BlockSpec((tq, w), lambda b, h, i: (b * nq + i, h)),
            kv_spec, kv_spec,
        ],
        out_specs=pl.BlockSpec((tq, w), lambda b, h, i: (b * nq + i, h)),
        out_shape=jax.ShapeDtypeStruct((M, D), BF16),
        compiler_params=_params(("arbitrary", "arbitrary", "arbitrary"), 32),
        name="attn",
    )(bias, q, k.reshape(B, T, D), v.reshape(B, T, D))


MXU_BF16_ROWS = 16


def _attn_paged_kernel(pt_ref, qbd_ref, bias_ref, kn_ref, vn_ref, *rest, n_pg, page, n_new,
                       n_heads, scale):
    k_refs, v_refs = rest[:n_pg], rest[n_pg:2 * n_pg]
    o_ref, acc_ref, tot_ref = rest[2 * n_pg:]
    s = pl.program_id(1)
    lanes = qbd_ref.shape[1]
    heads_per_group = MXU_BF16_ROWS // n_new
    row = lax.broadcasted_iota(jnp.int32, (page, page), 0)
    col = lax.broadcasted_iota(jnp.int32, (page, page), 1)
    tri = jnp.where(col >= row, 1.0, 0.0).astype(BF16)
    tri2 = jnp.concatenate([tri, tri], axis=1)

    head = lambda hd: slice(hd * HEAD_DIM, (hd + 1) * HEAD_DIM)
    def paged(ref):
        by_head = jnp.swapaxes(ref[...].astype(BF16).reshape(page, n_heads, HEAD_DIM), 0, 1)
        return lambda hd: by_head[hd]
    dense = lambda ref: (lambda hd: ref[:, head(hd)])

    def tiles(slabs, masked):
        zs = []
        for k_slab, _ in slabs:
            z = None
            for hd in range(n_heads):
                part = _dot(k_slab(hd), qbd_ref[head(hd), :])
                z = part if z is None else z + part
            zs.append(z * (scale * LOG2E) + bias_ref[...] * LOG2E)
        sps = [_softplus2(z) for z in zs]
        if masked:
            key = lax.broadcasted_iota(jnp.int32, (page, lanes), 0)
            qry = lax.broadcasted_iota(jnp.int32, (page, lanes), 1) % n_new
            mask = key < qry
            sps = [jnp.where(mask, sp, 0.0) for sp in sps]
        total = tot_ref[...]
        totals = []
        for sp in sps:
            totals.append(total)
            total = total + jnp.sum(sp, axis=0, keepdims=True)
        tot_ref[...] = total
        sums = [_dot(tri2, _split_bf16(sp, 0)) for sp in sps]
        probs = [jnp.exp2(z - sm - t) for z, sm, t in zip(zs, sums, totals)]
        if masked:
            probs = [jnp.where(mask, a, 0.0) for a in probs]
        pts = [a.T.astype(BF16) for a in probs]
        for hd in range(n_heads):
            g0 = (hd // heads_per_group) * MXU_BF16_ROWS
            pv = None
            for pt, (_, v_slab) in zip(pts, slabs):
                part = _dot(pt[g0:g0 + MXU_BF16_ROWS], v_slab(hd))
                pv = part if pv is None else pv + part
            acc_ref[hd] += pv

    @pl.when(s == 0)
    def _():
        acc_ref[...] = jnp.zeros_like(acc_ref)
        tot_ref[...] = jnp.zeros_like(tot_ref)
        tiles([(dense(kn_ref), dense(vn_ref))], True)

    tiles([(paged(k_refs[p]), paged(v_refs[p])) for p in range(n_pg)], False)

    @pl.when(s == pl.num_programs(1) - 1)
    def _():
        for hd in range(n_heads):
            r0 = (hd % heads_per_group) * n_new
            o_ref[:, head(hd)] = acc_ref[hd, r0:r0 + n_new, :]


def _attn_paged(q, k_new, v_new, cache_k, cache_v, page_table, bias, B, T, *, n_pg):
    M, D = q.shape
    n_phys, page, H, _ = cache_k.shape
    n_pages = page_table.shape[1]
    assert MXU_BF16_ROWS % T == 0 and n_pages % n_pg == 0
    n_qh = H * T
    lanes = V7X_LANES * pl.cdiv(n_qh, V7X_LANES)
    q4 = q.reshape(B, T, H, HEAD_DIM)
    qbd = jnp.einsum('bthd,hg->bhdgt', q4, jnp.eye(H, dtype=q.dtype)).reshape(B, D, n_qh)
    qbd = jnp.pad(qbd, ((0, 0), (0, 0), (0, lanes - n_qh)))
    bias_row = jnp.pad(jnp.repeat(bias.astype(F32), T), (0, lanes - n_qh)).reshape(1, lanes)
    pad_new = lambda a: jnp.pad(a.reshape(B, T, D), ((0, 0), (0, page - T), (0, 0)))

    ck = cache_k.reshape(n_phys, page * H, HEAD_DIM)
    cv = cache_v.reshape(n_phys, page * H, HEAD_DIM)

    def page_spec(p):
        return pl.BlockSpec((None, page * H, HEAD_DIM),
                            lambda b, s, pt: (pt[b, n_pages - 1 - (s * n_pg + p)], 0, 0))

    grid_spec = pltpu.PrefetchScalarGridSpec(
        num_scalar_prefetch=1,
        grid=(B, n_pages // n_pg),
        in_specs=[
            pl.BlockSpec((None, D, lanes), lambda b, s, pt: (b, 0, 0)),
            pl.BlockSpec((1, lanes), lambda b, s, pt: (0, 0)),
            pl.BlockSpec((None, page, D), lambda b, s, pt: (b, 0, 0)),
            pl.BlockSpec((None, page, D), lambda b, s, pt: (b, 0, 0)),
        ] + [page_spec(p) for p in range(n_pg)] * 2,
        out_specs=pl.BlockSpec((None, T, D), lambda b, s, pt: (b, 0, 0)),
        scratch_shapes=[pltpu.VMEM((H, MXU_BF16_ROWS, HEAD_DIM), F32),
                        pltpu.VMEM((1, lanes), F32)],
    )
    o = pl.pallas_call(
        functools.partial(_attn_paged_kernel, n_pg=n_pg, page=page, n_new=T, n_heads=H,
                          scale=HEAD_DIM ** -0.5),
        grid_spec=grid_spec,
        out_shape=jax.ShapeDtypeStruct((B, T, D), F32),
        compiler_params=_params(("arbitrary", "arbitrary"), 48),
        name="attn_paged",
    )(page_table, qbd, bias_row, pad_new(k_new), pad_new(v_new),
      *([ck] * n_pg), *([cv] * n_pg))
    return o.reshape(M, D).astype(BF16)


def _trunk(x3, mods_b, mods_kv_b, pos0, prev, past, wts, *, tm, tf, tm_proj, tn_proj, tn_oproj, tt,
           ffn_vmem):
    (norm_g, w_gate, w_up, w_down, w_pool, pool_scale, kv_norm_g, w_kv, k_norm_g, w_q, q_norm_g,
     sb_bias, w_o) = wts
    B, T, D = x3.shape
    M = B * T
    depth = norm_g.shape[0]
    n_a = w_pool.shape[0]
    per_row = T < tm
    if per_row:
        expand = lambda m: jnp.repeat(m.reshape(B, -1), T, axis=0)
    else:
        expand = lambda m: m
    tile = dict(tm=tm, rows_per_batch=T)
    ptile = dict(tm=tm_proj, rows_per_batch=T, vmem_mib=58)
    x = x3.reshape(M, D)
    gain = lambda a: a.reshape(1, D)
    states = []
    k16 = v16 = k32 = v32 = None
    for l in range(depth):
        mods = expand(mods_b[l])
        x = _ffn(x, mods, (0, 1, 2), gain(norm_g[l, 0]), w_gate, w_up, w_down, l, 0,
                 tf=tf, vmem_mib=ffn_vmem, **tile)
        if l < n_a:
            y3, st = _pool(x.reshape(B, T, D), mods_b[l], (3, 4, 5), gain(norm_g[l, 1]), prev[l],
                           w_pool[l], pool_scale[l], pos0, tt=tt)
            x = y3.reshape(M, D)
            states.append(st[:, POOL_HALO - POOL_STATE:])
        else:
            i = l - n_a
            (q,) = _proj(x, mods, (3, 4), gain(norm_g[l, 1]), w_q, i, 0, D, q_norm_g[i], (BF16,),
                         tn=tn_proj, **ptile)
            if past is None:
                o = _attn(q, k16, v16, sb_bias[i], B, T, tq=256, hps=8)
            else:
                o = _attn_paged(q, k16, v16, past[0], past[1], past[2], sb_bias[i], B, T, n_pg=4)
            x = _oproj(o, x, mods, 5, w_o, i, tn=tn_oproj, **ptile)
        x = _ffn(x, mods, (6, 7, 8), gain(norm_g[l, 2]), w_gate, w_up, w_down, l, 1,
                 tf=tf, vmem_mib=ffn_vmem, **tile)
        if l == n_a - 1:
            mkv = expand(mods_kv_b)
            k32, k16 = _proj(x, mkv, (0, 1), gain(kv_norm_g), w_kv[None], 0, 0, D, k_norm_g,
                             (F32, BF16), tn=tn_proj, **ptile)
            v32, v16 = _proj(x, mkv, (0, 1), gain(kv_norm_g), w_kv[None], 0, D, D, None,
                             (F32, BF16), tn=tn_proj, **ptile)
    H = D // HEAD_DIM
    return (x.reshape(B, T, D), k32.reshape(B, T, H, HEAD_DIM), v32.reshape(B, T, H, HEAD_DIM),
            jnp.stack(states))


def kernel(x_prompt, x_sample, c_prompt, c_sample, cache_k, cache_v, state_pool, page_table, norm_g, w_ada, b_ada, w_gate, w_up, w_down, w_pool, pool_scale, kv_norm_g, w_ada_kv, b_ada_kv, w_kv, k_norm_g, w_q, q_norm_g, sb_bias, w_o):
    Bp, Tp, D = x_prompt.shape
    Bs, Ts, _ = x_sample.shape
    n_a = w_pool.shape[0]
    past_len = page_table.shape[1] * cache_k.shape[1]

    c_all = jnp.concatenate([c_prompt, c_sample], axis=0)
    rows = V7X_SUBLANES * 2 * pl.cdiv(Bp + Bs, V7X_SUBLANES * 2)
    c_all = jnp.pad(c_all, ((0, rows - (Bp + Bs)), (0, 0)))
    mods = _ada(c_all, w_ada, b_ada, tn=1024)
    mods_kv = _ada(c_all, w_ada_kv[None], b_ada_kv[None], tn=1024)[0]
    split = lambda m, lo, n: m[..., lo:lo + n, :][..., None, :]

    wts = (norm_g, w_gate, w_up, w_down, w_pool, pool_scale, kv_norm_g, w_kv, k_norm_g, w_q,
           q_norm_g, sb_bias, w_o)

    zero_prev = jnp.zeros((n_a, Bp, POOL_HALO, D), x_prompt.dtype)
    y_p, k_p, v_p, pool_p = _trunk(
        x_prompt, split(mods, 0, Bp), split(mods_kv, 0, Bp), 0, zero_prev, None, wts,
        tm=1024, tf=256, tm_proj=2048, tn_proj=256, tn_oproj=512, tt=512, ffn_vmem=56)

    prev_s = jnp.pad(state_pool, ((0, 0), (0, 0), (POOL_HALO - POOL_STATE, 0), (0, 0)))
    y_s, k_s, v_s, pool_s = _trunk(
        x_sample, split(mods, Bp, Bs), split(mods_kv, Bp, Bs), past_len, prev_s,
        (cache_k, cache_v, page_table), wts,
        tm=Bs * Ts, tf=512, tm_proj=Bs * Ts, tn_proj=512, tn_oproj=512, tt=Ts, ffn_vmem=48)

    return (y_p, y_s, k_p, v_p, k_s, v_s, pool_p, pool_s)
```

```python
import functools

import jax
import jax.numpy as jnp
from jax import lax
from jax.experimental import pallas as pl
from jax.experimental.pallas import tpu as pltpu

V7X_LANES = 128
V7X_SUBLANES = 8
V7X_VMEM_BYTES = 64 * 1024 * 1024

POOL_WINDOWS = (2, 4, 8, 16)
POOL_STATE = max(POOL_WINDOWS) - 1
POOL_HALO = POOL_STATE + 1
HEAD_DIM = 128
N_MOD = 9
EPS = 1e-6
MXU_BF16_ROWS = 16
RIDER_ROWS = MXU_BF16_ROWS
FFN_DOWN_COLS = 512

BF16 = jnp.bfloat16
F32 = jnp.float32


def _dot(a, b):
    return jnp.dot(a, b, preferred_element_type=F32)


def _dot_nt(a, b):
    return lax.dot_general(a, b, (((1,), (1,)), ((), ())), preferred_element_type=F32)


def _norm_mod(x, g, shift, scale):
    ms = jnp.mean(x * x, axis=-1, keepdims=True)
    return x * lax.rsqrt(ms + EPS) * (g * (1.0 + scale)) + shift


def _params(sem, vmem_mib):
    return pltpu.CompilerParams(dimension_semantics=sem,
                                vmem_limit_bytes=vmem_mib * 1024 * 1024)


def _mod_spec(mods, j, tm, rows_per_batch, width, ncol=1):
    if ncol == 1:
        col = lambda g: j
    else:
        col = lambda g: j * ncol + g[1]
    if mods.ndim == 3:
        tiles_per_batch = rows_per_batch // tm
        return pl.BlockSpec((None, 1, width), lambda *g: (g[0] // tiles_per_batch, 0, col(g)))
    return pl.BlockSpec((tm, width), lambda *g: (g[0], col(g)))


def _ada_kernel(c_ref, w_ref, b_ref, o_ref):
    c = c_ref[...]
    sc = (c * jax.nn.sigmoid(c)).astype(BF16)
    o_ref[...] = _dot(sc, w_ref[...].astype(BF16)) + b_ref[...]


def _ada(c, w, b, tn):
    L, D, N = w.shape
    R = c.shape[0]
    return pl.pallas_call(
        _ada_kernel,
        grid=(L, N // tn),
        in_specs=[
            pl.BlockSpec((R, D), lambda l, n: (0, 0)),
            pl.BlockSpec((None, D, tn), lambda l, n: (l, 0, n)),
            pl.BlockSpec((None, 1, tn), lambda l, n: (l, 0, n)),
        ],
        out_specs=pl.BlockSpec((None, R, tn), lambda l, n: (l, 0, n)),
        out_shape=jax.ShapeDtypeStruct((L, R, N), F32),
        compiler_params=_params(("arbitrary", "arbitrary"), 40),
        name="ada",
    )(c, w, b.reshape(L, 1, N))


def _ffn_kernel(*refs, nf, tf, last, n_parts, down_cols):
    xs = refs[:n_parts]
    g_ref = refs[n_parts]
    mods = [refs[n_parts + 1 + 3 * p:n_parts + 4 + 3 * p] for p in range(n_parts)]
    wg_ref, wu_ref, wd_ref = refs[4 * n_parts + 1:4 * n_parts + 4]
    outs = refs[4 * n_parts + 4:5 * n_parts + 4]
    h_ref = refs[5 * n_parts + 4]
    f = pl.program_id(1)
    rows, r0 = [], 0
    for x_ref in xs:
        rows.append(slice(r0, r0 + x_ref.shape[0]))
        r0 += x_ref.shape[0]
    D = wd_ref.shape[1]

    @pl.when(f == 0)
    def _():
        for x_ref, (sh_ref, sc_ref, _), o_ref, rs in zip(xs, mods, outs, rows):
            h_ref[rs, :] = _norm_mod(x_ref[...], g_ref[...], sh_ref[...], sc_ref[...]).astype(BF16)
            o_ref[...] = jnp.zeros_like(o_ref)

    def step(width):
        h = h_ref[...]
        gate = _dot(h, wg_ref[:, :width].astype(BF16))
        up = _dot(h, wu_ref[:, :width].astype(BF16))
        a = (gate * jax.nn.sigmoid(gate) * up).astype(BF16)
        for c0 in range(0, D, down_cols):
            cs = slice(c0, c0 + down_cols)
            r = _dot(a, wd_ref[:width, cs].astype(BF16))
            for o_ref, rs in zip(outs, rows):
                o_ref[:, cs] += r[rs]

    if last == tf:
        step(tf)
    else:
        pl.when(f < nf - 1)(lambda: step(tf))
        pl.when(f == nf - 1)(lambda: step(last))

    @pl.when(f == nf - 1)
    def _():
        for x_ref, (_, _, gt_ref), o_ref in zip(xs, mods, outs):
            o_ref[...] = x_ref[...] + 0.5 * gt_ref[...] * o_ref[...]


def _ffn(x, mods, midx, g, w_gate, w_up, w_down, l, j, *, tm, tf, rows_per_batch, vmem_mib,
         rider=None):
    M, D = x.shape
    F = w_gate.shape[-1]
    nf = pl.cdiv(F, tf)
    last = F - (nf - 1) * tf
    ms = functools.partial(_mod_spec, mods, tm=tm, rows_per_batch=rows_per_batch, width=D)
    parts = [(x, pl.BlockSpec((tm, D), lambda i, f: (i, 0)), [ms(j=k) for k in midx], mods)]
    if rider is not None:
        xr, mods_r = rider
        assert xr.shape[0] == (M // tm) * RIDER_ROWS and mods_r.shape[0] == M // tm
        mr = lambda k: pl.BlockSpec((None, 1, D), lambda i, f: (i, 0, k))
        parts.append((xr, pl.BlockSpec((RIDER_ROWS, D), lambda i, f: (i, 0)),
                      [mr(k) for k in midx], mods_r))
    n_parts = len(parts)
    wspec = lambda shape, idx: pl.BlockSpec((None, None) + shape, idx)
    outs = pl.pallas_call(
        functools.partial(_ffn_kernel, nf=nf, tf=tf, last=last, n_parts=n_parts,
                          down_cols=min(D, FFN_DOWN_COLS)),
        grid=(M // tm, nf),
        in_specs=[p[1] for p in parts] + [pl.BlockSpec((1, D), lambda i, f: (0, 0))]
        + [spec for p in parts for spec in p[2]] + [
            wspec((D, tf), lambda i, f: (l, j, 0, f)),
            wspec((D, tf), lambda i, f: (l, j, 0, f)),
            wspec((tf, D), lambda i, f: (l, j, f, 0)),
        ],
        out_specs=[p[1] for p in parts],
        out_shape=[jax.ShapeDtypeStruct(p[0].shape, F32) for p in parts],
        scratch_shapes=[pltpu.VMEM((sum(p[1].block_shape[0] for p in parts), D), BF16)],
        compiler_params=_params(("arbitrary", "arbitrary"), vmem_mib),
        name="ffn",
    )(*[p[0] for p in parts], g, *[p[3] for p in parts for _ in range(3)], w_gate, w_up, w_down)
    return outs[0] if rider is None else tuple(outs)


def _proj_kernel(x_ref, g_ref, sh_ref, sc_ref, w_ref, hg_ref, *rest, head_norm, n_out):
    outs, h_ref = rest[:n_out], rest[n_out]
    n = pl.program_id(1)

    @pl.when(n == 0)
    def _():
        h_ref[...] = _norm_mod(x_ref[...], g_ref[...], sh_ref[...], sc_ref[...]).astype(BF16)

    y = _dot(h_ref[...], w_ref[...].astype(BF16))
    if head_norm:
        hg = hg_ref[...]
        for c in range(y.shape[1] // HEAD_DIM):
            sl = slice(c * HEAD_DIM, (c + 1) * HEAD_DIM)
            yh = y[:, sl]
            ms = jnp.mean(yh * yh, axis=-1, keepdims=True)
            yn = yh * lax.rsqrt(ms + EPS) * hg
            for o in outs:
                o[:, sl] = yn.astype(o.dtype)
    else:
        for o in outs:
            o[...] = y.astype(o.dtype)


def _proj(x, mods, midx, g, w, wi, col0, n_cols, head_g, out_dtypes, *, tm, tn, rows_per_batch,
          vmem_mib):
    M, D = x.shape
    ms = functools.partial(_mod_spec, mods, tm=tm, rows_per_batch=rows_per_batch, width=D)
    cb = col0 // tn
    head_norm = head_g is not None
    if head_g is None:
        head_g = jnp.ones((HEAD_DIM,), F32)
    outs = pl.pallas_call(
        functools.partial(_proj_kernel, head_norm=head_norm, n_out=len(out_dtypes)),
        grid=(M // tm, n_cols // tn),
        in_specs=[
            pl.BlockSpec((tm, D), lambda i, n: (i, 0)),
            pl.BlockSpec((1, D), lambda i, n: (0, 0)),
            ms(j=midx[0]), ms(j=midx[1]),
            pl.BlockSpec((None, D, tn), lambda i, n: (wi, 0, cb + n)),
            pl.BlockSpec((1, HEAD_DIM), lambda i, n: (0, 0)),
        ],
        out_specs=[pl.BlockSpec((tm, tn), lambda i, n: (i, n)) for _ in out_dtypes],
        out_shape=[jax.ShapeDtypeStruct((M, n_cols), dt) for dt in out_dtypes],
        scratch_shapes=[pltpu.VMEM((tm, D), BF16)],
        compiler_params=_params(("arbitrary", "arbitrary"), vmem_mib),
        name="proj",
    )(x, g, mods, mods, w, head_g.reshape(1, HEAD_DIM))
    return outs


def _oproj_kernel(o_ref, w_ref, x_ref, gt_ref, y_ref):
    y_ref[...] = x_ref[...] + gt_ref[...] * _dot(o_ref[...], w_ref[...].astype(BF16))


def _oproj(o, x, mods, gidx, w, wi, *, tm, tn, rows_per_batch, vmem_mib):
    M, D = x.shape
    return pl.pallas_call(
        _oproj_kernel,
        grid=(M // tm, D // tn),
        in_specs=[
            pl.BlockSpec((tm, D), lambda i, n: (i, 0)),
            pl.BlockSpec((None, D, tn), lambda i, n: (wi, 0, n)),
            pl.BlockSpec((tm, tn), lambda i, n: (i, n)),
            _mod_spec(mods, gidx, tm, rows_per_batch, tn, ncol=D // tn),
        ],
        out_specs=pl.BlockSpec((tm, tn), lambda i, n: (i, n)),
        out_shape=jax.ShapeDtypeStruct((M, D), F32),
        compiler_params=_params(("arbitrary", "arbitrary"), vmem_mib),
        name="oproj",
    )(o, w, x, mods)


def _pool_kernel(x_ref, g_ref, sh_ref, sc_ref, gt_ref, prev_ref, w_ref, ps_ref, y_ref, st_ref,
                 buf_ref, *, tt, pos0):
    t = pl.program_id(1)
    D = x_ref.shape[-1]
    group = D // len(POOL_WINDOWS)

    @pl.when(t == 0)
    def _():
        buf_ref[0:POOL_HALO, :] = prev_ref[...]

    @pl.when(t > 0)
    def _():
        buf_ref[0:POOL_HALO, :] = buf_ref[tt:tt + POOL_HALO, :]

    x = x_ref[...]
    buf_ref[POOL_HALO:POOL_HALO + tt, :] = _norm_mod(x, g_ref[...], sh_ref[...], sc_ref[...])
    st_ref[...] = buf_ref[tt:tt + POOL_HALO, :]

    pos = (pos0 + t * tt + lax.broadcasted_iota(jnp.int32, (tt, 1), 0)).astype(F32)
    for gi, w in enumerate(POOL_WINDOWS):
        sl = slice(gi * group, (gi + 1) * group)
        h = buf_ref[POOL_HALO:POOL_HALO + tt, sl]
        wsum = h
        for back in range(1, w):
            wsum = wsum + buf_ref[POOL_HALO - back:POOL_HALO - back + tt, sl]
        cnt = jnp.minimum(float(w), pos + 1.0)
        diff = (wsum / cnt - h).astype(BF16)
        m = _dot(diff, w_ref[gi].astype(BF16)) * ps_ref[:, sl]
        y_ref[:, sl] = x[:, sl] + gt_ref[:, sl] * m


def _pool(x3, mods_b, midx, g, prev, w_pool_l, pool_scale_l, pos0, *, tt):
    B, T, D = x3.shape
    ms = lambda j: pl.BlockSpec((None, 1, D), lambda b, t: (b, 0, j))
    return pl.pallas_call(
        functools.partial(_pool_kernel, tt=tt, pos0=pos0),
        grid=(B, T // tt),
        in_specs=[
            pl.BlockSpec((None, tt, D), lambda b, t: (b, t, 0)),
            pl.BlockSpec((1, D), lambda b, t: (0, 0)),
            ms(midx[0]), ms(midx[1]), ms(midx[2]),
            pl.BlockSpec((None, POOL_HALO, D), lambda b, t: (b, 0, 0)),
            pl.BlockSpec(w_pool_l.shape, lambda b, t: (0, 0, 0)),
            pl.BlockSpec((1, D), lambda b, t: (0, 0)),
        ],
        out_specs=[
            pl.BlockSpec((None, tt, D), lambda b, t: (b, t, 0)),
            pl.BlockSpec((None, POOL_HALO, D), lambda b, t: (b, 0, 0)),
        ],
        out_shape=[jax.ShapeDtypeStruct((B, T, D), F32),
                   jax.ShapeDtypeStruct((B, POOL_HALO, D), F32)],
        scratch_shapes=[pltpu.VMEM((POOL_HALO + tt, D), F32)],
        compiler_params=_params(("arbitrary", "arbitrary"), 48),
        name="pool",
    )(x3, g, mods_b, mods_b, mods_b, prev, w_pool_l, pool_scale_l.reshape(1, D))


LOG2E = 1.4426950408889634


def _softplus2(y):
    neg_abs = lax.bitcast_convert_type(
        lax.bitcast_convert_type(y, jnp.uint32) | jnp.uint32(0x80000000), F32)
    return jnp.maximum(y, 0.0) + jnp.log2(1.0 + jnp.exp2(neg_abs))


def _split_bf16(x, axis):
    hi = x.astype(BF16)
    lo = (x - hi.astype(F32)).astype(BF16)
    return jnp.concatenate([hi, lo], axis=axis)


def _attn_kernel(bias_ref, q_ref, k_ref, v_ref, o_ref, *, tq, hps, scale):
    hg = pl.program_id(1)
    i = pl.program_id(2)
    row = lax.broadcasted_iota(jnp.int32, (tq, tq), 0)
    col = lax.broadcasted_iota(jnp.int32, (tq, tq), 1)
    tri = jnp.where(row >= col, 1.0, 0.0).astype(BF16)
    tri2 = jnp.concatenate([tri, tri], axis=0)
    causal = col < row
    heads = [slice(c * HEAD_DIM, (c + 1) * HEAD_DIM) for c in range(hps)]
    bias = [bias_ref[hg * hps + c] * LOG2E for c in range(hps)]

    def tiles(k0, totals, mask):
        zs = [_dot_nt(q_ref[:, hd], k_ref[pl.ds(k0, tq), hd]) * (scale * LOG2E) + b
              for hd, b in zip(heads, bias)]
        sps = [_softplus2(z) for z in zs]
        if mask is not None:
            sps = [jnp.where(mask, sp, 0.0) for sp in sps]
        sums = [_dot(_split_bf16(sp, 1), tri2) for sp in sps]
        probs = [jnp.exp2(z - sm) for z, sm in zip(zs, sums)]
        if mask is not None:
            probs = [jnp.where(mask, a, 0.0) for a in probs]
        pvs = [_dot(a.astype(BF16), v_ref[pl.ds(k0, tq), hd]) * jnp.exp2(-total)
               for a, hd, total in zip(probs, heads, totals)]
        totals = [total + sm[:, :1] for total, sm in zip(totals, sums)]
        return pvs, totals

    d0 = pl.multiple_of(i * tq, tq)
    accs, totals = tiles(d0, [jnp.zeros((tq, 1), F32)] * hps, causal)

    def body(n, carry):
        accs, totals = carry
        k0 = pl.multiple_of((i - 1 - n) * tq, tq)
        pvs, totals = tiles(k0, totals, None)
        return [acc + pv for acc, pv in zip(accs, pvs)], totals

    accs, totals = lax.fori_loop(0, i, body, (accs, totals))
    for acc, hd in zip(accs, heads):
        o_ref[:, hd] = acc.astype(o_ref.dtype)


def _attn(q, k, v, bias, B, T, *, tq, hps):
    M, D = q.shape
    H = D // HEAD_DIM
    nq = T // tq
    w = hps * HEAD_DIM
    kv_spec = pl.BlockSpec((None, T, w), lambda b, h, i: (b, 0, h))
    return pl.pallas_call(
        functools.partial(_attn_kernel, tq=tq, hps=hps, scale=HEAD_DIM ** -0.5),
        grid=(B, H // hps, nq),
        in_specs=[
            pl.BlockSpec(memory_space=pltpu.SMEM),
            pl.BlockSpec((tq, w), lambda b, h, i: (b * nq + i, h)),
            kv_spec, kv_spec,
        ],
        out_specs=pl.BlockSpec((tq, w), lambda b, h, i: (b * nq + i, h)),
        out_shape=jax.ShapeDtypeStruct((M, D), BF16),
        compiler_params=_params(("arbitrary", "arbitrary", "arbitrary"), 32),
        name="attn",
    )(bias, q, k.reshape(B, T, D), v.reshape(B, T, D))


def _attn_paged_kernel(pt_ref, qbd_ref, bias_ref, kn_ref, vn_ref, *rest, n_pg, page, n_new,
                       n_heads, scale):
    k_refs, v_refs = rest[:n_pg], rest[n_pg:2 * n_pg]
    o_ref, acc_ref, tot_ref = rest[2 * n_pg:]
    s = pl.program_id(1)
    lanes = qbd_ref.shape[1]
    heads_per_group = MXU_BF16_ROWS // n_new
    row = lax.broadcasted_iota(jnp.int32, (page, page), 0)
    col = lax.broadcasted_iota(jnp.int32, (page, page), 1)
    tri = jnp.where(col >= row, 1.0, 0.0).astype(BF16)
    tri2 = jnp.concatenate([tri, tri], axis=1)

    head = lambda hd: slice(hd * HEAD_DIM, (hd + 1) * HEAD_DIM)
    def paged(ref):
        by_head = jnp.swapaxes(ref[...].astype(BF16).reshape(page, n_heads, HEAD_DIM), 0, 1)
        return lambda hd: by_head[hd]
    dense = lambda ref: (lambda hd: ref[:, head(hd)])

    def tiles(slabs, masked):
        zs = []
        for k_slab, _ in slabs:
            z = None
            for hd in range(n_heads):
                part = _dot(k_slab(hd), qbd_ref[head(hd), :])
                z = part if z is None else z + part
            zs.append(z * (scale * LOG2E) + bias_ref[...] * LOG2E)
        sps = [_softplus2(z) for z in zs]
        if masked:
            key = lax.broadcasted_iota(jnp.int32, (page, lanes), 0)
            qry = lax.broadcasted_iota(jnp.int32, (page, lanes), 1) % n_new
            mask = key < qry
            sps = [jnp.where(mask, sp, 0.0) for sp in sps]
        total = tot_ref[...]
        totals = []
        for sp in sps:
            totals.append(total)
            total = total + jnp.sum(sp, axis=0, keepdims=True)
        tot_ref[...] = total
        sums = [_dot(tri2, _split_bf16(sp, 0)) for sp in sps]
        probs = [jnp.exp2(z - sm - t) for z, sm, t in zip(zs, sums, totals)]
        if masked:
            probs = [jnp.where(mask, a, 0.0) for a in probs]
        pts = [a.T.astype(BF16) for a in probs]
        for hd in range(n_heads):
            g0 = (hd // heads_per_group) * MXU_BF16_ROWS
            pv = None
            for pt, (_, v_slab) in zip(pts, slabs):
                part = _dot(pt[g0:g0 + MXU_BF16_ROWS], v_slab(hd))
                pv = part if pv is None else pv + part
            acc_ref[hd] += pv

    @pl.when(s == 0)
    def _():
        acc_ref[...] = jnp.zeros_like(acc_ref)
        tot_ref[...] = jnp.zeros_like(tot_ref)
        tiles([(dense(kn_ref), dense(vn_ref))], True)

    tiles([(paged(k_refs[p]), paged(v_refs[p])) for p in range(n_pg)], False)

    @pl.when(s == pl.num_programs(1) - 1)
    def _():
        for hd in range(n_heads):
            r0 = (hd % heads_per_group) * n_new
            o_ref[:, head(hd)] = acc_ref[hd, r0:r0 + n_new, :]


def _attn_paged(q, k_new, v_new, cache_k, cache_v, page_table, bias, B, T, *, n_pg):
    M, D = q.shape
    n_phys, page, H, _ = cache_k.shape
    n_pages = page_table.shape[1]
    assert MXU_BF16_ROWS % T == 0 and n_pages % n_pg == 0
    n_qh = H * T
    lanes = V7X_LANES * pl.cdiv(n_qh, V7X_LANES)
    q4 = q.reshape(B, T, H, HEAD_DIM)
    qbd = jnp.einsum('bthd,hg->bhdgt', q4, jnp.eye(H, dtype=q.dtype)).reshape(B, D, n_qh)
    qbd = jnp.pad(qbd, ((0, 0), (0, 0), (0, lanes - n_qh)))
    bias_row = jnp.pad(jnp.repeat(bias.astype(F32), T), (0, lanes - n_qh)).reshape(1, lanes)
    pad_new = lambda a: jnp.pad(a.reshape(B, T, D), ((0, 0), (0, page - T), (0, 0)))

    ck = cache_k.reshape(n_phys, page * H, HEAD_DIM)
    cv = cache_v.reshape(n_phys, page * H, HEAD_DIM)

    def page_spec(p):
        return pl.BlockSpec((None, page * H, HEAD_DIM),
                            lambda b, s, pt: (pt[b, n_pages - 1 - (s * n_pg + p)], 0, 0))

    grid_spec = pltpu.PrefetchScalarGridSpec(
        num_scalar_prefetch=1,
        grid=(B, n_pages // n_pg),
        in_specs=[
            pl.BlockSpec((None, D, lanes), lambda b, s, pt: (b, 0, 0)),
            pl.BlockSpec((1, lanes), lambda b, s, pt: (0, 0)),
            pl.BlockSpec((None, page, D), lambda b, s, pt: (b, 0, 0)),
            pl.BlockSpec((None, page, D), lambda b, s, pt: (b, 0, 0)),
        ] + [page_spec(p) for p in range(n_pg)] * 2,
        out_specs=pl.BlockSpec((None, T, D), lambda b, s, pt: (b, 0, 0)),
        scratch_shapes=[pltpu.VMEM((H, MXU_BF16_ROWS, HEAD_DIM), F32),
                        pltpu.VMEM((1, lanes), F32)],
    )
    o = pl.pallas_call(
        functools.partial(_attn_paged_kernel, n_pg=n_pg, page=page, n_new=T, n_heads=H,
                          scale=HEAD_DIM ** -0.5),
        grid_spec=grid_spec,
        out_shape=jax.ShapeDtypeStruct((B, T, D), F32),
        compiler_params=_params(("arbitrary", "arbitrary"), 48),
        name="attn_paged",
    )(page_table, qbd, bias_row, pad_new(k_new), pad_new(v_new),
      *([ck] * n_pg), *([cv] * n_pg))
    return o.reshape(M, D).astype(BF16)


class _Group:
    def __init__(self, x3, mods_b, mods_kv_b, pos0, prev, past, *, tm, tf, tm_proj, tn_proj,
                 tn_oproj, tt, ffn_vmem):
        self.B, self.T, self.D = x3.shape
        self.x = x3.reshape(self.B * self.T, self.D)
        self.mods_b, self.mods_kv_b = mods_b, mods_kv_b
        self.pos0, self.prev, self.past = pos0, prev, past
        self.tt, self.tn_proj, self.tn_oproj = tt, tn_proj, tn_oproj
        self.ffn_tile = dict(tm=tm, rows_per_batch=self.T, tf=tf, vmem_mib=ffn_vmem)
        self.proj_tile = dict(tm=tm_proj, rows_per_batch=self.T, vmem_mib=58)
        self.per_row = self.T < tm
        self.states = []
        self.k16 = self.v16 = self.k32 = self.v32 = None

    def mods(self, m):
        return jnp.repeat(m.reshape(self.B, -1), self.T, axis=0) if self.per_row else m


def _ffn_both(main, side, l, j, midx, g, w_gate, w_up, w_down):
    n_tiles = main.x.shape[0] // main.ffn_tile["tm"]
    if n_tiles == side.B and side.T <= RIDER_ROWS:
        B, T, D = side.B, side.T, side.D
        xr = jnp.pad(side.x.reshape(B, T, D), ((0, 0), (0, RIDER_ROWS - T), (0, 0)))
        main.x, yr = _ffn(main.x, main.mods(main.mods_b[l]), midx, g, w_gate, w_up, w_down, l, j,
                          rider=(xr.reshape(B * RIDER_ROWS, D), side.mods_b[l]), **main.ffn_tile)
        side.x = yr.reshape(B, RIDER_ROWS, D)[:, :T].reshape(B * T, D)
    else:
        for gp in (main, side):
            gp.x = _ffn(gp.x, gp.mods(gp.mods_b[l]), midx, g, w_gate, w_up, w_down, l, j,
                        **gp.ffn_tile)


def _mixer(gp, l, wts):
    (norm_g, _, _, _, w_pool, pool_scale, _, _, _, w_q, q_norm_g, sb_bias, w_o) = wts
    B, T, D = gp.B, gp.T, gp.D
    n_a = w_pool.shape[0]
    g = norm_g[l, 1].reshape(1, D)
    if l < n_a:
        y3, st = _pool(gp.x.reshape(B, T, D), gp.mods_b[l], (3, 4, 5), g, gp.prev[l], w_pool[l],
                       pool_scale[l], gp.pos0, tt=gp.tt)
        gp.x = y3.reshape(B * T, D)
        gp.states.append(st[:, POOL_HALO - POOL_STATE:])
        return
    i = l - n_a
    mods = gp.mods(gp.mods_b[l])
    (q,) = _proj(gp.x, mods, (3, 4), g, w_q, i, 0, D, q_norm_g[i], (BF16,), tn=gp.tn_proj,
                 **gp.proj_tile)
    if gp.past is None:
        o = _attn(q, gp.k16, gp.v16, sb_bias[i], B, T, tq=256, hps=8)
    else:
        o = _attn_paged(q, gp.k16, gp.v16, *gp.past, sb_bias[i], B, T, n_pg=4)
    gp.x = _oproj(o, gp.x, mods, 5, w_o, i, tn=gp.tn_oproj, **gp.proj_tile)


def _shared_kv(gp, wts):
    (_, _, _, _, _, _, kv_norm_g, w_kv, k_norm_g, _, _, _, _) = wts
    D = gp.D
    mkv = gp.mods(gp.mods_kv_b)
    g = kv_norm_g.reshape(1, D)
    gp.k32, gp.k16 = _proj(gp.x, mkv, (0, 1), g, w_kv[None], 0, 0, D, k_norm_g, (F32, BF16),
                           tn=gp.tn_proj, **gp.proj_tile)
    gp.v32, gp.v16 = _proj(gp.x, mkv, (0, 1), g, w_kv[None], 0, D, D, None, (F32, BF16),
                           tn=gp.tn_proj, **gp.proj_tile)


def _trunks(main, side, wts):
    norm_g, w_gate, w_up, w_down, w_pool = wts[:5]
    depth, n_a, D = norm_g.shape[0], w_pool.shape[0], main.D
    for l in range(depth):
        _ffn_both(main, side, l, 0, (0, 1, 2), norm_g[l, 0].reshape(1, D), w_gate, w_up, w_down)
        for gp in (main, side):
            _mixer(gp, l, wts)
        _ffn_both(main, side, l, 1, (6, 7, 8), norm_g[l, 2].reshape(1, D), w_gate, w_up, w_down)
        if l == n_a - 1:
            for gp in (main, side):
                _shared_kv(gp, wts)
    H = D // HEAD_DIM
    heads = lambda gp, a: a.reshape(gp.B, gp.T, H, HEAD_DIM)
    return [(gp.x.reshape(gp.B, gp.T, D), heads(gp, gp.k32), heads(gp, gp.v32),
             jnp.stack(gp.states)) for gp in (main, side)]


def kernel(x_prompt, x_sample, c_prompt, c_sample, cache_k, cache_v, state_pool, page_table, norm_g, w_ada, b_ada, w_gate, w_up, w_down, w_pool, pool_scale, kv_norm_g, w_ada_kv, b_ada_kv, w_kv, k_norm_g, w_q, q_norm_g, sb_bias, w_o):
    Bp, Tp, D = x_prompt.shape
    Bs, Ts, _ = x_sample.shape
    n_a = w_pool.shape[0]
    past_len = page_table.shape[1] * cache_k.shape[1]

    c_all = jnp.concatenate([c_prompt, c_sample], axis=0)
    rows = V7X_SUBLANES * 2 * pl.cdiv(Bp + Bs, V7X_SUBLANES * 2)
    c_all = jnp.pad(c_all, ((0, rows - (Bp + Bs)), (0, 0)))
    mods = _ada(c_all, w_ada, b_ada, tn=1024)
    mods_kv = _ada(c_all, w_ada_kv[None], b_ada_kv[None], tn=1024)[0]
    split = lambda m, lo, n: m[..., lo:lo + n, :][..., None, :]

    wts = (norm_g, w_gate, w_up, w_down, w_pool, pool_scale, kv_norm_g, w_kv, k_norm_g, w_q,
           q_norm_g, sb_bias, w_o)

    zero_prev = jnp.zeros((n_a, Bp, POOL_HALO, D), x_prompt.dtype)
    prompt = _Group(x_prompt, split(mods, 0, Bp), split(mods_kv, 0, Bp), 0, zero_prev, None,
                    tm=1024, tf=256, tm_proj=2048, tn_proj=256, tn_oproj=512, tt=512, ffn_vmem=58)
    prev_s = jnp.pad(state_pool, ((0, 0), (0, 0), (POOL_HALO - POOL_STATE, 0), (0, 0)))
    sample = _Group(x_sample, split(mods, Bp, Bs), split(mods_kv, Bp, Bs), past_len, prev_s,
                    (cache_k, cache_v, page_table),
                    tm=Bs * Ts, tf=512, tm_proj=Bs * Ts, tn_proj=512, tn_oproj=512, tt=Ts,
                    ffn_vmem=48)
    (y_p, k_p, v_p, pool_p), (y_s, k_s, v_s, pool_s) = _trunks(prompt, sample, wts)
    return (y_p, y_s, k_p, v_p, k_s, v_s, pool_p, pool_s)
```

```python
import functools

import jax
import jax.numpy as jnp
from jax import lax
from jax.experimental import pallas as pl
from jax.experimental.pallas import tpu as pltpu

V7X_LANES = 128
V7X_SUBLANES = 8
V7X_VMEM_BYTES = 64 * 1024 * 1024

POOL_WINDOWS = (2, 4, 8, 16)
POOL_STATE = max(POOL_WINDOWS) - 1
POOL_HALO = POOL_STATE + 1
HEAD_DIM = 128
N_MOD = 9
EPS = 1e-6
MXU_BF16_ROWS = 16
RIDER_ROWS = MXU_BF16_ROWS
FFN_DOWN_COLS = 512

BF16 = jnp.bfloat16
F32 = jnp.float32


def _dot(a, b):
    return jnp.dot(a, b, preferred_element_type=F32)


def _dot_nt(a, b):
    return lax.dot_general(a, b, (((1,), (1,)), ((), ())), preferred_element_type=F32)


def _norm_mod(x, g, shift, scale):
    ms = jnp.mean(x * x, axis=-1, keepdims=True)
    return x * lax.rsqrt(ms + EPS) * (g * (1.0 + scale)) + shift


def _params(sem, vmem_mib):
    return pltpu.CompilerParams(dimension_semantics=sem,
                                vmem_limit_bytes=vmem_mib * 1024 * 1024)


def _mod_spec(mods, j, tm, rows_per_batch, width, ncol=1):
    if ncol == 1:
        col = lambda g: j
    else:
        col = lambda g: j * ncol + g[1]
    if mods.ndim == 3:
        tiles_per_batch = rows_per_batch // tm
        return pl.BlockSpec((None, 1, width), lambda *g: (g[0] // tiles_per_batch, 0, col(g)))
    return pl.BlockSpec((tm, width), lambda *g: (g[0], col(g)))


def _ada_kernel(c_ref, w_ref, b_ref, o_ref):
    c = c_ref[...]
    sc = (c * jax.nn.sigmoid(c)).astype(BF16)
    o_ref[...] = _dot(sc, w_ref[...].astype(BF16)) + b_ref[...]


def _ada(c, w, b, tn):
    L, D, N = w.shape
    R = c.shape[0]
    return pl.pallas_call(
        _ada_kernel,
        grid=(L, N // tn),
        in_specs=[
            pl.BlockSpec((R, D), lambda l, n: (0, 0)),
            pl.BlockSpec((None, D, tn), lambda l, n: (l, 0, n)),
            pl.BlockSpec((None, 1, tn), lambda l, n: (l, 0, n)),
        ],
        out_specs=pl.BlockSpec((None, R, tn), lambda l, n: (l, 0, n)),
        out_shape=jax.ShapeDtypeStruct((L, R, N), F32),
        compiler_params=_params(("arbitrary", "arbitrary"), 40),
        name="ada",
    )(c, w, b.reshape(L, 1, N))


def _ffn_kernel(*refs, nf, tf, last, n_parts, down_cols):
    xs = refs[:n_parts]
    g_ref = refs[n_parts]
    mods = [refs[n_parts + 1 + 3 * p:n_parts + 4 + 3 * p] for p in range(n_parts)]
    wg_ref, wu_ref, wd_ref = refs[4 * n_parts + 1:4 * n_parts + 4]
    outs = refs[4 * n_parts + 4:5 * n_parts + 4]
    h_ref = refs[5 * n_parts + 4]
    f = pl.program_id(1)
    rows, r0 = [], 0
    for x_ref in xs:
        rows.append(slice(r0, r0 + x_ref.shape[0]))
        r0 += x_ref.shape[0]
    D = wd_ref.shape[1]

    @pl.when(f == 0)
    def _():
        for x_ref, (sh_ref, sc_ref, _), o_ref, rs in zip(xs, mods, outs, rows):
            h_ref[rs, :] = _norm_mod(x_ref[...], g_ref[...], sh_ref[...], sc_ref[...]).astype(BF16)

    def step(width, first):
        h = h_ref[...]
        gate = _dot(h, wg_ref[:, :width].astype(BF16))
        up = _dot(h, wu_ref[:, :width].astype(BF16))
        a = (gate * jax.nn.sigmoid(gate) * up).astype(BF16)
        for c0 in range(0, D, down_cols):
            cs = slice(c0, c0 + down_cols)
            r = _dot(a, wd_ref[:width, cs].astype(BF16))
            for o_ref, rs in zip(outs, rows):
                if first:
                    o_ref[:, cs] = r[rs]
                else:
                    o_ref[:, cs] += r[rs]

    assert nf >= 2
    pl.when(f == 0)(lambda: step(tf, True))
    pl.when((f > 0) & (f < nf - 1))(lambda: step(tf, False))
    pl.when(f == nf - 1)(lambda: step(last, False))

    @pl.when(f == nf - 1)
    def _():
        for x_ref, (_, _, gt_ref), o_ref in zip(xs, mods, outs):
            o_ref[...] = x_ref[...] + 0.5 * gt_ref[...] * o_ref[...]


def _ffn(x, mods, midx, g, w_gate, w_up, w_down, l, j, *, tm, tf, rows_per_batch, vmem_mib,
         rider=None):
    M, D = x.shape
    F = w_gate.shape[-1]
    nf = pl.cdiv(F, tf)
    last = F - (nf - 1) * tf
    ms = functools.partial(_mod_spec, mods, tm=tm, rows_per_batch=rows_per_batch, width=D)
    parts = [(x, pl.BlockSpec((tm, D), lambda i, f: (i, 0)), [ms(j=k) for k in midx], mods)]
    if rider is not None:
        xr, mods_r = rider
        assert xr.shape[0] == (M // tm) * RIDER_ROWS and mods_r.shape[0] == M // tm
        mr = lambda k: pl.BlockSpec((None, 1, D), lambda i, f: (i, 0, k))
        parts.append((xr, pl.BlockSpec((RIDER_ROWS, D), lambda i, f: (i, 0)),
                      [mr(k) for k in midx], mods_r))
    n_parts = len(parts)
    wspec = lambda shape, idx: pl.BlockSpec((None, None) + shape, idx)
    outs = pl.pallas_call(
        functools.partial(_ffn_kernel, nf=nf, tf=tf, last=last, n_parts=n_parts,
                          down_cols=min(D, FFN_DOWN_COLS)),
        grid=(M // tm, nf),
        in_specs=[p[1] for p in parts] + [pl.BlockSpec((1, D), lambda i, f: (0, 0))]
        + [spec for p in parts for spec in p[2]] + [
            wspec((D, tf), lambda i, f: (l, j, 0, f)),
            wspec((D, tf), lambda i, f: (l, j, 0, f)),
            wspec((tf, D), lambda i, f: (l, j, f, 0)),
        ],
        out_specs=[p[1] for p in parts],
        out_shape=[jax.ShapeDtypeStruct(p[0].shape, F32) for p in parts],
        scratch_shapes=[pltpu.VMEM((sum(p[1].block_shape[0] for p in parts), D), BF16)],
        compiler_params=_params(("arbitrary", "arbitrary"), vmem_mib),
        name="ffn",
    )(*[p[0] for p in parts], g, *[p[3] for p in parts for _ in range(3)], w_gate, w_up, w_down)
    return outs[0] if rider is None else tuple(outs)


def _proj_kernel(x_ref, g_ref, sh_ref, sc_ref, w_ref, hg_ref, *rest, head_norm, n_out):
    outs, h_ref = rest[:n_out], rest[n_out]
    n = pl.program_id(1)

    def columns():
        y = _dot(h_ref[...], w_ref[...].astype(BF16))
        if head_norm:
            hg = hg_ref[...]
            for c in range(y.shape[1] // HEAD_DIM):
                sl = slice(c * HEAD_DIM, (c + 1) * HEAD_DIM)
                yh = y[:, sl]
                ms = jnp.mean(yh * yh, axis=-1, keepdims=True)
                yn = yh * lax.rsqrt(ms + EPS) * hg
                for o in outs:
                    o[:, sl] = yn.astype(o.dtype)
        else:
            for o in outs:
                o[...] = y.astype(o.dtype)

    @pl.when(n == 0)
    def _():
        h_ref[...] = _norm_mod(x_ref[...], g_ref[...], sh_ref[...], sc_ref[...]).astype(BF16)
        columns()

    pl.when(n > 0)(columns)


def _proj(x, mods, midx, g, w, wi, col0, n_cols, head_g, out_dtypes, *, tm, tn, rows_per_batch,
          vmem_mib):
    M, D = x.shape
    ms = functools.partial(_mod_spec, mods, tm=tm, rows_per_batch=rows_per_batch, width=D)
    cb = col0 // tn
    head_norm = head_g is not None
    if head_g is None:
        head_g = jnp.ones((HEAD_DIM,), F32)
    outs = pl.pallas_call(
        functools.partial(_proj_kernel, head_norm=head_norm, n_out=len(out_dtypes)),
        grid=(M // tm, n_cols // tn),
        in_specs=[
            pl.BlockSpec((tm, D), lambda i, n: (i, 0)),
            pl.BlockSpec((1, D), lambda i, n: (0, 0)),
            ms(j=midx[0]), ms(j=midx[1]),
            pl.BlockSpec((None, D, tn), lambda i, n: (wi, 0, cb + n)),
            pl.BlockSpec((1, HEAD_DIM), lambda i, n: (0, 0)),
        ],
        out_specs=[pl.BlockSpec((tm, tn), lambda i, n: (i, n)) for _ in out_dtypes],
        out_shape=[jax.ShapeDtypeStruct((M, n_cols), dt) for dt in out_dtypes],
        scratch_shapes=[pltpu.VMEM((tm, D), BF16)],
        compiler_params=_params(("arbitrary", "arbitrary"), vmem_mib),
        name="proj",
    )(x, g, mods, mods, w, head_g.reshape(1, HEAD_DIM))
    return outs


def _oproj_kernel(o_ref, w_ref, x_ref, gt_ref, y_ref):
    y_ref[...] = x_ref[...] + gt_ref[...] * _dot(o_ref[...], w_ref[...].astype(BF16))


def _oproj(o, x, mods, gidx, w, wi, *, tm, tn, rows_per_batch, vmem_mib):
    M, D = x.shape
    return pl.pallas_call(
        _oproj_kernel,
        grid=(M // tm, D // tn),
        in_specs=[
            pl.BlockSpec((tm, D), lambda i, n: (i, 0)),
            pl.BlockSpec((None, D, tn), lambda i, n: (wi, 0, n)),
            pl.BlockSpec((tm, tn), lambda i, n: (i, n)),
            _mod_spec(mods, gidx, tm, rows_per_batch, tn, ncol=D // tn),
        ],
        out_specs=pl.BlockSpec((tm, tn), lambda i, n: (i, n)),
        out_shape=jax.ShapeDtypeStruct((M, D), F32),
        compiler_params=_params(("arbitrary", "arbitrary"), vmem_mib),
        name="oproj",
    )(o, w, x, mods)


def _pool_kernel(x_ref, g_ref, sh_ref, sc_ref, gt_ref, prev_ref, w_ref, ps_ref, y_ref, st_ref,
                 buf_ref, *, tt, pos0):
    t = pl.program_id(1)
    D = x_ref.shape[-1]
    group = D // len(POOL_WINDOWS)

    @pl.when(t == 0)
    def _():
        buf_ref[0:POOL_HALO, :] = prev_ref[...]

    @pl.when(t > 0)
    def _():
        buf_ref[0:POOL_HALO, :] = buf_ref[tt:tt + POOL_HALO, :]

    x = x_ref[...]
    buf_ref[POOL_HALO:POOL_HALO + tt, :] = _norm_mod(x, g_ref[...], sh_ref[...], sc_ref[...])
    st_ref[...] = buf_ref[tt:tt + POOL_HALO, :]

    pos = (pos0 + t * tt + lax.broadcasted_iota(jnp.int32, (tt, 1), 0)).astype(F32)
    for gi, w in enumerate(POOL_WINDOWS):
        sl = slice(gi * group, (gi + 1) * group)
        h = buf_ref[POOL_HALO:POOL_HALO + tt, sl]
        wsum = h
        for back in range(1, w):
            wsum = wsum + buf_ref[POOL_HALO - back:POOL_HALO - back + tt, sl]
        cnt = jnp.minimum(float(w), pos + 1.0)
        diff = (wsum / cnt - h).astype(BF16)
        m = _dot(diff, w_ref[gi].astype(BF16)) * ps_ref[:, sl]
        y_ref[:, sl] = x[:, sl] + gt_ref[:, sl] * m


def _pool(x3, mods_b, midx, g, prev, w_pool_l, pool_scale_l, pos0, *, tt):
    B, T, D = x3.shape
    ms = lambda j: pl.BlockSpec((None, 1, D), lambda b, t: (b, 0, j))
    return pl.pallas_call(
        functools.partial(_pool_kernel, tt=tt, pos0=pos0),
        grid=(B, T // tt),
        in_specs=[
            pl.BlockSpec((None, tt, D), lambda b, t: (b, t, 0)),
            pl.BlockSpec((1, D), lambda b, t: (0, 0)),
            ms(midx[0]), ms(midx[1]), ms(midx[2]),
            pl.BlockSpec((None, POOL_HALO, D), lambda b, t: (b, 0, 0)),
            pl.BlockSpec(w_pool_l.shape, lambda b, t: (0, 0, 0)),
            pl.BlockSpec((1, D), lambda b, t: (0, 0)),
        ],
        out_specs=[
            pl.BlockSpec((None, tt, D), lambda b, t: (b, t, 0)),
            pl.BlockSpec((None, POOL_HALO, D), lambda b, t: (b, 0, 0)),
        ],
        out_shape=[jax.ShapeDtypeStruct((B, T, D), F32),
                   jax.ShapeDtypeStruct((B, POOL_HALO, D), F32)],
        scratch_shapes=[pltpu.VMEM((POOL_HALO + tt, D), F32)],
        compiler_params=_params(("arbitrary", "arbitrary"), 48),
        name="pool",
    )(x3, g, mods_b, mods_b, mods_b, prev, w_pool_l, pool_scale_l.reshape(1, D))


LOG2E = 1.4426950408889634


def _softplus2(y):
    neg_abs = lax.bitcast_convert_type(
        lax.bitcast_convert_type(y, jnp.uint32) | jnp.uint32(0x80000000), F32)
    return jnp.maximum(y, 0.0) + jnp.log2(1.0 + jnp.exp2(neg_abs))


def _split_bf16(x, axis):
    hi = x.astype(BF16)
    lo = (x - hi.astype(F32)).astype(BF16)
    return jnp.concatenate([hi, lo], axis=axis)


def _attn_kernel(bias_ref, q_ref, k_ref, v_ref, o_ref, *, tq, hps, scale):
    hg = pl.program_id(1)
    i = pl.program_id(2)
    row = lax.broadcasted_iota(jnp.int32, (tq, tq), 0)
    col = lax.broadcasted_iota(jnp.int32, (tq, tq), 1)
    tri = jnp.where(row >= col, 1.0, 0.0).astype(BF16)
    tri2 = jnp.concatenate([tri, tri], axis=0)
    causal = col < row
    heads = [slice(c * HEAD_DIM, (c + 1) * HEAD_DIM) for c in range(hps)]
    bias = [bias_ref[hg * hps + c] * LOG2E for c in range(hps)]

    def tiles(k0, totals, mask):
        zs = [_dot_nt(q_ref[:, hd], k_ref[pl.ds(k0, tq), hd]) * (scale * LOG2E) + b
              for hd, b in zip(heads, bias)]
        sps = [_softplus2(z) for z in zs]
        if mask is not None:
            sps = [jnp.where(mask, sp, 0.0) for sp in sps]
        sums = [_dot(_split_bf16(sp, 1), tri2) for sp in sps]
        probs = [jnp.exp2(z - sm) for z, sm in zip(zs, sums)]
        if mask is not None:
            probs = [jnp.where(mask, a, 0.0) for a in probs]
        pvs = [_dot(a.astype(BF16), v_ref[pl.ds(k0, tq), hd]) * jnp.exp2(-total)
               for a, hd, total in zip(probs, heads, totals)]
        totals = [total + sm[:, :1] for total, sm in zip(totals, sums)]
        return pvs, totals

    d0 = pl.multiple_of(i * tq, tq)
    accs, totals = tiles(d0, [jnp.zeros((tq, 1), F32)] * hps, causal)

    def body(n, carry):
        accs, totals = carry
        k0 = pl.multiple_of((i - 1 - n) * tq, tq)
        pvs, totals = tiles(k0, totals, None)
        return [acc + pv for acc, pv in zip(accs, pvs)], totals

    accs, totals = lax.fori_loop(0, i, body, (accs, totals))
    for acc, hd in zip(accs, heads):
        o_ref[:, hd] = acc.astype(o_ref.dtype)


def _attn(q, k, v, bias, B, T, *, tq, hps):
    M, D = q.shape
    H = D // HEAD_DIM
    nq = T // tq
    w = hps * HEAD_DIM
    kv_spec = pl.BlockSpec((None, T, w), lambda b, h, i: (b, 0, h))
    return pl.pallas_call(
        functools.partial(_attn_kernel, tq=tq, hps=hps, scale=HEAD_DIM ** -0.5),
        grid=(B, H // hps, nq),
        in_specs=[
            pl.BlockSpec(memory_space=pltpu.SMEM),
            pl.BlockSpec((tq, w), lambda b, h, i: (b * nq + i, h)),
            kv_spec, kv_spec,
        ],
        out_specs=pl.BlockSpec((tq, w), lambda b, h, i: (b * nq + i, h)),
        out_shape=jax.ShapeDtypeStruct((M, D), BF16),
        compiler_params=_params(("arbitrary", "arbitrary", "arbitrary"), 32),
        name="attn",
    )(bias, q, k.reshape(B, T, D), v.reshape(B, T, D))


def _attn_paged_kernel(pt_ref, qbd_ref, bias_ref, kn_ref, vn_ref, *rest, n_pg, page, n_new,
                       n_heads, scale):
    k_refs, v_refs = rest[:n_pg], rest[n_pg:2 * n_pg]
    o_ref, acc_ref, tot_ref = rest[2 * n_pg:]
    s = pl.program_id(1)
    lanes = qbd_ref.shape[1]
    heads_per_group = MXU_BF16_ROWS // n_new
    row = lax.broadcasted_iota(jnp.int32, (page, page), 0)
    col = lax.broadcasted_iota(jnp.int32, (page, page), 1)
    tri = jnp.where(col >= row, 1.0, 0.0).astype(BF16)
    tri2 = jnp.concatenate([tri, tri], axis=1)

    head = lambda hd: slice(hd * HEAD_DIM, (hd + 1) * HEAD_DIM)
    def paged(ref):
        by_head = jnp.swapaxes(ref[...].astype(BF16).reshape(page, n_heads, HEAD_DIM), 0, 1)
        return lambda hd: by_head[hd]
    dense = lambda ref: (lambda hd: ref[:, head(hd)])

    def tiles(slabs, masked):
        zs = []
        for k_slab, _ in slabs:
            z = None
            for hd in range(n_heads):
                part = _dot(k_slab(hd), qbd_ref[head(hd), :])
                z = part if z is None else z + part
            zs.append(z * (scale * LOG2E) + bias_ref[...] * LOG2E)
        sps = [_softplus2(z) for z in zs]
        if masked:
            key = lax.broadcasted_iota(jnp.int32, (page, lanes), 0)
            qry = lax.broadcasted_iota(jnp.int32, (page, lanes), 1) % n_new
            mask = key < qry
            sps = [jnp.where(mask, sp, 0.0) for sp in sps]
        total = tot_ref[...]
        totals = []
        for sp in sps:
            totals.append(total)
            total = total + jnp.sum(sp, axis=0, keepdims=True)
        tot_ref[...] = total
        sums = [_dot(tri2, _split_bf16(sp, 0)) for sp in sps]
        probs = [jnp.exp2(z - sm - t) for z, sm, t in zip(zs, sums, totals)]
        if masked:
            probs = [jnp.where(mask, a, 0.0) for a in probs]
        pts = [a.T.astype(BF16) for a in probs]
        for hd in range(n_heads):
            g0 = (hd // heads_per_group) * MXU_BF16_ROWS
            pv = None
            for pt, (_, v_slab) in zip(pts, slabs):
                part = _dot(pt[g0:g0 + MXU_BF16_ROWS], v_slab(hd))
                pv = part if pv is None else pv + part
            acc_ref[hd] += pv

    @pl.when(s == 0)
    def _():
        acc_ref[...] = jnp.zeros_like(acc_ref)
        tot_ref[...] = jnp.zeros_like(tot_ref)
        tiles([(dense(kn_ref), dense(vn_ref))], True)

    tiles([(paged(k_refs[p]), paged(v_refs[p])) for p in range(n_pg)], False)

    @pl.when(s == pl.num_programs(1) - 1)
    def _():
        for hd in range(n_heads):
            r0 = (hd % heads_per_group) * n_new
            o_ref[:, head(hd)] = acc_ref[hd, r0:r0 + n_new, :]


def _attn_paged(q, k_new, v_new, cache_k, cache_v, page_table, bias, B, T, *, n_pg):
    M, D = q.shape
    n_phys, page, H, _ = cache_k.shape
    n_pages = page_table.shape[1]
    assert MXU_BF16_ROWS % T == 0 and n_pages % n_pg == 0
    n_qh = H * T
    lanes = V7X_LANES * pl.cdiv(n_qh, V7X_LANES)
    q4 = q.reshape(B, T, H, HEAD_DIM)
    qbd = jnp.einsum('bthd,hg->bhdgt', q4, jnp.eye(H, dtype=q.dtype)).reshape(B, D, n_qh)
    qbd = jnp.pad(qbd, ((0, 0), (0, 0), (0, lanes - n_qh)))
    bias_row = jnp.pad(jnp.repeat(bias.astype(F32), T), (0, lanes - n_qh)).reshape(1, lanes)
    pad_new = lambda a: jnp.pad(a.reshape(B, T, D), ((0, 0), (0, page - T), (0, 0)))

    ck = cache_k.reshape(n_phys, page * H, HEAD_DIM)
    cv = cache_v.reshape(n_phys, page * H, HEAD_DIM)

    def page_spec(p):
        return pl.BlockSpec((None, page * H, HEAD_DIM),
                            lambda b, s, pt: (pt[b, n_pages - 1 - (s * n_pg + p)], 0, 0))

    grid_spec = pltpu.PrefetchScalarGridSpec(
        num_scalar_prefetch=1,
        grid=(B, n_pages // n_pg),
        in_specs=[
            pl.BlockSpec((None, D, lanes), lambda b, s, pt: (b, 0, 0)),
            pl.BlockSpec((1, lanes), lambda b, s, pt: (0, 0)),
            pl.BlockSpec((None, page, D), lambda b, s, pt: (b, 0, 0)),
            pl.BlockSpec((None, page, D), lambda b, s, pt: (b, 0, 0)),
        ] + [page_spec(p) for p in range(n_pg)] * 2,
        out_specs=pl.BlockSpec((None, T, D), lambda b, s, pt: (b, 0, 0)),
        scratch_shapes=[pltpu.VMEM((H, MXU_BF16_ROWS, HEAD_DIM), F32),
                        pltpu.VMEM((1, lanes), F32)],
    )
    o = pl.pallas_call(
        functools.partial(_attn_paged_kernel, n_pg=n_pg, page=page, n_new=T, n_heads=H,
                          scale=HEAD_DIM ** -0.5),
        grid_spec=grid_spec,
        out_shape=jax.ShapeDtypeStruct((B, T, D), F32),
        compiler_params=_params(("arbitrary", "arbitrary"), 48),
        name="attn_paged",
    )(page_table, qbd, bias_row, pad_new(k_new), pad_new(v_new),
      *([ck] * n_pg), *([cv] * n_pg))
    return o.reshape(M, D).astype(BF16)


class _Group:
    def __init__(self, x3, mods_b, mods_kv_b, pos0, prev, past, *, tm, tf, tm_proj, tn_proj,
                 tn_oproj, tt, ffn_vmem):
        self.B, self.T, self.D = x3.shape
        self.x = x3.reshape(self.B * self.T, self.D)
        self.mods_b, self.mods_kv_b = mods_b, mods_kv_b
        self.pos0, self.prev, self.past = pos0, prev, past
        self.tt, self.tn_proj, self.tn_oproj = tt, tn_proj, tn_oproj
        self.ffn_tile = dict(tm=tm, rows_per_batch=self.T, tf=tf, vmem_mib=ffn_vmem)
        self.proj_tile = dict(tm=tm_proj, rows_per_batch=self.T, vmem_mib=58)
        self.per_row = self.T < tm
        self.states = []
        self.k16 = self.v16 = self.k32 = self.v32 = None

    def mods(self, m):
        return jnp.repeat(m.reshape(self.B, -1), self.T, axis=0) if self.per_row else m


def _ffn_both(main, side, l, j, midx, g, w_gate, w_up, w_down):
    n_tiles = main.x.shape[0] // main.ffn_tile["tm"]
    if n_tiles == side.B and side.T <= RIDER_ROWS:
        B, T, D = side.B, side.T, side.D
        xr = jnp.pad(side.x.reshape(B, T, D), ((0, 0), (0, RIDER_ROWS - T), (0, 0)))
        main.x, yr = _ffn(main.x, main.mods(main.mods_b[l]), midx, g, w_gate, w_up, w_down, l, j,
                          rider=(xr.reshape(B * RIDER_ROWS, D), side.mods_b[l]), **main.ffn_tile)
        side.x = yr.reshape(B, RIDER_ROWS, D)[:, :T].reshape(B * T, D)
    else:
        for gp in (main, side):
            gp.x = _ffn(gp.x, gp.mods(gp.mods_b[l]), midx, g, w_gate, w_up, w_down, l, j,
                        **gp.ffn_tile)


def _mixer(gp, l, wts):
    (norm_g, _, _, _, w_pool, pool_scale, _, _, _, w_q, q_norm_g, sb_bias, w_o) = wts
    B, T, D = gp.B, gp.T, gp.D
    n_a = w_pool.shape[0]
    g = norm_g[l, 1].reshape(1, D)
    if l < n_a:
        y3, st = _pool(gp.x.reshape(B, T, D), gp.mods_b[l], (3, 4, 5), g, gp.prev[l], w_pool[l],
                       pool_scale[l], gp.pos0, tt=gp.tt)
        gp.x = y3.reshape(B * T, D)
        gp.states.append(st[:, POOL_HALO - POOL_STATE:])
        return
    i = l - n_a
    mods = gp.mods(gp.mods_b[l])
    (q,) = _proj(gp.x, mods, (3, 4), g, w_q, i, 0, D, q_norm_g[i], (BF16,), tn=gp.tn_proj,
                 **gp.proj_tile)
    if gp.past is None:
        o = _attn(q, gp.k16, gp.v16, sb_bias[i], B, T, tq=256, hps=8)
    else:
        o = _attn_paged(q, gp.k16, gp.v16, *gp.past, sb_bias[i], B, T, n_pg=8)
    gp.x = _oproj(o, gp.x, mods, 5, w_o, i, tn=gp.tn_oproj, **gp.proj_tile)


def _shared_kv(gp, wts):
    (_, _, _, _, _, _, kv_norm_g, w_kv, k_norm_g, _, _, _, _) = wts
    D = gp.D
    mkv = gp.mods(gp.mods_kv_b)
    g = kv_norm_g.reshape(1, D)
    gp.k32, gp.k16 = _proj(gp.x, mkv, (0, 1), g, w_kv[None], 0, 0, D, k_norm_g, (F32, BF16),
                           tn=gp.tn_proj, **gp.proj_tile)
    gp.v32, gp.v16 = _proj(gp.x, mkv, (0, 1), g, w_kv[None], 0, D, D, None, (F32, BF16),
                           tn=gp.tn_proj, **gp.proj_tile)


def _trunks(main, side, wts):
    norm_g, w_gate, w_up, w_down, w_pool = wts[:5]
    depth, n_a, D = norm_g.shape[0], w_pool.shape[0], main.D
    for l in range(depth):
        _ffn_both(main, side, l, 0, (0, 1, 2), norm_g[l, 0].reshape(1, D), w_gate, w_up, w_down)
        for gp in (main, side):
            _mixer(gp, l, wts)
        _ffn_both(main, side, l, 1, (6, 7, 8), norm_g[l, 2].reshape(1, D), w_gate, w_up, w_down)
        if l == n_a - 1:
            for gp in (main, side):
                _shared_kv(gp, wts)
    H = D // HEAD_DIM
    heads = lambda gp, a: a.reshape(gp.B, gp.T, H, HEAD_DIM)
    return [(gp.x.reshape(gp.B, gp.T, D), heads(gp, gp.k32), heads(gp, gp.v32),
             jnp.stack(gp.states)) for gp in (main, side)]


def kernel(x_prompt, x_sample, c_prompt, c_sample, cache_k, cache_v, state_pool, page_table, norm_g, w_ada, b_ada, w_gate, w_up, w_down, w_pool, pool_scale, kv_norm_g, w_ada_kv, b_ada_kv, w_kv, k_norm_g, w_q, q_norm_g, sb_bias, w_o):
    Bp, Tp, D = x_prompt.shape
    Bs, Ts, _ = x_sample.shape
    n_a = w_pool.shape[0]
    past_len = page_table.shape[1] * cache_k.shape[1]

    c_all = jnp.concatenate([c_prompt, c_sample], axis=0)
    rows = V7X_SUBLANES * 2 * pl.cdiv(Bp + Bs, V7X_SUBLANES * 2)
    c_all = jnp.pad(c_all, ((0, rows - (Bp + Bs)), (0, 0)))
    mods = _ada(c_all, w_ada, b_ada, tn=1024)
    mods_kv = _ada(c_all, w_ada_kv[None], b_ada_kv[None], tn=1024)[0]
    split = lambda m, lo, n: m[..., lo:lo + n, :][..., None, :]

    wts = (norm_g, w_gate, w_up, w_down, w_pool, pool_scale, kv_norm_g, w_kv, k_norm_g, w_q,
           q_norm_g, sb_bias, w_o)

    zero_prev = jnp.zeros((n_a, Bp, POOL_HALO, D), x_prompt.dtype)
    prompt = _Group(x_prompt, split(mods, 0, Bp), split(mods_kv, 0, Bp), 0, zero_prev, None,
                    tm=1024, tf=256, tm_proj=2048, tn_proj=256, tn_oproj=512, tt=512, ffn_vmem=58)
    prev_s = jnp.pad(state_pool, ((0, 0), (0, 0), (POOL_HALO - POOL_STATE, 0), (0, 0)))
    sample = _Group(x_sample, split(mods, Bp, Bs), split(mods_kv, Bp, Bs), past_len, prev_s,
                    (cache_k, cache_v, page_table),
                    tm=Bs * Ts, tf=512, tm_proj=Bs * Ts, tn_proj=512, tn_oproj=512, tt=Ts,
                    ffn_vmem=48)
    (y_p, k_p, v_p, pool_p), (y_s, k_s, v_s, pool_s) = _trunks(prompt, sample, wts)
    return (y_p, y_s, k_p, v_p, k_s, v_s, pool_p, pool_s)
```

```python
import functools

import jax
import jax.numpy as jnp
from jax import lax
from jax.experimental import pallas as pl
from jax.experimental.pallas import tpu as pltpu

V7X_LANES = 128
V7X_SUBLANES = 8
V7X_VMEM_BYTES = 64 * 1024 * 1024

POOL_WINDOWS = (2, 4, 8, 16)
POOL_STATE = max(POOL_WINDOWS) - 1
POOL_HALO = POOL_STATE + 1
HEAD_DIM = 128
N_MOD = 9
EPS = 1e-6
MXU_BF16_ROWS = 16
RIDER_ROWS = MXU_BF16_ROWS
FFN_DOWN_COLS = 512

BF16 = jnp.bfloat16
F32 = jnp.float32


def _dot(a, b):
    return jnp.dot(a, b, preferred_element_type=F32)


def _dot_nt(a, b):
    return lax.dot_general(a, b, (((1,), (1,)), ((), ())), preferred_element_type=F32)


def _norm_mod(x, g, shift, scale):
    ms = jnp.mean(x * x, axis=-1, keepdims=True)
    return x * lax.rsqrt(ms + EPS) * (g * (1.0 + scale)) + shift


def _params(sem, vmem_mib):
    return pltpu.CompilerParams(dimension_semantics=sem,
                                vmem_limit_bytes=vmem_mib * 1024 * 1024)


def _mod_spec(mods, j, tm, rows_per_batch, width, ncol=1):
    if ncol == 1:
        col = lambda g: j
    else:
        col = lambda g: j * ncol + g[1]
    if mods.ndim == 3:
        tiles_per_batch = rows_per_batch // tm
        return pl.BlockSpec((None, 1, width), lambda *g: (g[0] // tiles_per_batch, 0, col(g)))
    return pl.BlockSpec((tm, width), lambda *g: (g[0], col(g)))


def _ada_kernel(c_ref, w_ref, b_ref, o_ref):
    c = c_ref[...]
    sc = (c * jax.nn.sigmoid(c)).astype(BF16)
    o_ref[...] = _dot(sc, w_ref[...].astype(BF16)) + b_ref[...]


def _ada(c, w, b, tn):
    L, D, N = w.shape
    R = c.shape[0]
    return pl.pallas_call(
        _ada_kernel,
        grid=(L, N // tn),
        in_specs=[
            pl.BlockSpec((R, D), lambda l, n: (0, 0)),
            pl.BlockSpec((None, D, tn), lambda l, n: (l, 0, n)),
            pl.BlockSpec((None, 1, tn), lambda l, n: (l, 0, n)),
        ],
        out_specs=pl.BlockSpec((None, R, tn), lambda l, n: (l, 0, n)),
        out_shape=jax.ShapeDtypeStruct((L, R, N), F32),
        compiler_params=_params(("arbitrary", "arbitrary"), 40),
        name="ada",
    )(c, w, b.reshape(L, 1, N))


def _ffn_kernel(*refs, nf, tf, last, n_parts, down_cols):
    xs = refs[:n_parts]
    g_ref = refs[n_parts]
    mods = [refs[n_parts + 1 + 3 * p:n_parts + 4 + 3 * p] for p in range(n_parts)]
    wg_ref, wu_ref, wd_ref = refs[4 * n_parts + 1:4 * n_parts + 4]
    outs = refs[4 * n_parts + 4:5 * n_parts + 4]
    h_ref = refs[5 * n_parts + 4]
    f = pl.program_id(1)
    rows, r0 = [], 0
    for x_ref in xs:
        rows.append(slice(r0, r0 + x_ref.shape[0]))
        r0 += x_ref.shape[0]
    D = wd_ref.shape[1]

    @pl.when(f == 0)
    def _():
        for x_ref, (sh_ref, sc_ref, _), o_ref, rs in zip(xs, mods, outs, rows):
            h_ref[rs, :] = _norm_mod(x_ref[...], g_ref[...], sh_ref[...], sc_ref[...]).astype(BF16)

    def step(width, first=False, final=False):
        h = h_ref[...]
        gate = _dot(h, wg_ref[:, :width].astype(BF16))
        up = _dot(h, wu_ref[:, :width].astype(BF16))
        a = (gate * jax.nn.sigmoid(gate) * up).astype(BF16)
        for c0 in range(0, D, down_cols):
            cs = slice(c0, c0 + down_cols)
            r = _dot(a, wd_ref[:width, cs].astype(BF16))
            for x_ref, (_, _, gt_ref), o_ref, rs in zip(xs, mods, outs, rows):
                if first:
                    o_ref[:, cs] = r[rs]
                elif final:
                    o_ref[:, cs] = x_ref[:, cs] + 0.5 * gt_ref[:, cs] * (o_ref[:, cs] + r[rs])
                else:
                    o_ref[:, cs] += r[rs]

    assert nf >= 2
    pl.when(f == 0)(lambda: step(tf, first=True))
    pl.when((f > 0) & (f < nf - 1))(lambda: step(tf))
    pl.when(f == nf - 1)(lambda: step(last, final=True))


def _ffn(x, mods, midx, g, w_gate, w_up, w_down, l, j, *, tm, tf, rows_per_batch, vmem_mib,
         rider=None):
    M, D = x.shape
    F = w_gate.shape[-1]
    nf = pl.cdiv(F, tf)
    last = F - (nf - 1) * tf
    ms = functools.partial(_mod_spec, mods, tm=tm, rows_per_batch=rows_per_batch, width=D)
    parts = [(x, pl.BlockSpec((tm, D), lambda i, f: (i, 0)), [ms(j=k) for k in midx], mods)]
    if rider is not None:
        xr, mods_r = rider
        assert xr.shape[0] == (M // tm) * RIDER_ROWS and mods_r.shape[0] == M // tm
        mr = lambda k: pl.BlockSpec((None, 1, D), lambda i, f: (i, 0, k))
        parts.append((xr, pl.BlockSpec((RIDER_ROWS, D), lambda i, f: (i, 0)),
                      [mr(k) for k in midx], mods_r))
    n_parts = len(parts)
    wspec = lambda shape, idx: pl.BlockSpec((None, None) + shape, idx)
    outs = pl.pallas_call(
        functools.partial(_ffn_kernel, nf=nf, tf=tf, last=last, n_parts=n_parts,
                          down_cols=min(D, FFN_DOWN_COLS)),
        grid=(M // tm, nf),
        in_specs=[p[1] for p in parts] + [pl.BlockSpec((1, D), lambda i, f: (0, 0))]
        + [spec for p in parts for spec in p[2]] + [
            wspec((D, tf), lambda i, f: (l, j, 0, f)),
            wspec((D, tf), lambda i, f: (l, j, 0, f)),
            wspec((tf, D), lambda i, f: (l, j, f, 0)),
        ],
        out_specs=[p[1] for p in parts],
        out_shape=[jax.ShapeDtypeStruct(p[0].shape, F32) for p in parts],
        scratch_shapes=[pltpu.VMEM((sum(p[1].block_shape[0] for p in parts), D), BF16)],
        compiler_params=_params(("arbitrary", "arbitrary"), vmem_mib),
        name="ffn",
    )(*[p[0] for p in parts], g, *[p[3] for p in parts for _ in range(3)], w_gate, w_up, w_down)
    return outs[0] if rider is None else tuple(outs)


def _proj_kernel(x_ref, g_ref, sh_ref, sc_ref, w_ref, hg_ref, *rest, head_norm, n_out):
    outs, h_ref = rest[:n_out], rest[n_out]
    n = pl.program_id(1)

    def columns():
        y = _dot(h_ref[...], w_ref[...].astype(BF16))
        if head_norm:
            hg = hg_ref[...]
            for c in range(y.shape[1] // HEAD_DIM):
                sl = slice(c * HEAD_DIM, (c + 1) * HEAD_DIM)
                yh = y[:, sl]
                ms = jnp.mean(yh * yh, axis=-1, keepdims=True)
                yn = yh * lax.rsqrt(ms + EPS) * hg
                for o in outs:
                    o[:, sl] = yn.astype(o.dtype)
        else:
            for o in outs:
                o[...] = y.astype(o.dtype)

    @pl.when(n == 0)
    def _():
        h_ref[...] = _norm_mod(x_ref[...], g_ref[...], sh_ref[...], sc_ref[...]).astype(BF16)
        columns()

    pl.when(n > 0)(columns)


def _proj(x, mods, midx, g, w, wi, col0, n_cols, head_g, out_dtypes, *, tm, tn, rows_per_batch,
          vmem_mib):
    M, D = x.shape
    ms = functools.partial(_mod_spec, mods, tm=tm, rows_per_batch=rows_per_batch, width=D)
    cb = col0 // tn
    head_norm = head_g is not None
    if head_g is None:
        head_g = jnp.ones((HEAD_DIM,), F32)
    outs = pl.pallas_call(
        functools.partial(_proj_kernel, head_norm=head_norm, n_out=len(out_dtypes)),
        grid=(M // tm, n_cols // tn),
        in_specs=[
            pl.BlockSpec((tm, D), lambda i, n: (i, 0)),
            pl.BlockSpec((1, D), lambda i, n: (0, 0)),
            ms(j=midx[0]), ms(j=midx[1]),
            pl.BlockSpec((None, D, tn), lambda i, n: (wi, 0, cb + n)),
            pl.BlockSpec((1, HEAD_DIM), lambda i, n: (0, 0)),
        ],
        out_specs=[pl.BlockSpec((tm, tn), lambda i, n: (i, n)) for _ in out_dtypes],
        out_shape=[jax.ShapeDtypeStruct((M, n_cols), dt) for dt in out_dtypes],
        scratch_shapes=[pltpu.VMEM((tm, D), BF16)],
        compiler_params=_params(("arbitrary", "arbitrary"), vmem_mib),
        name="proj",
    )(x, g, mods, mods, w, head_g.reshape(1, HEAD_DIM))
    return outs


def _oproj_kernel(o_ref, w_ref, x_ref, gt_ref, y_ref):
    y_ref[...] = x_ref[...] + gt_ref[...] * _dot(o_ref[...], w_ref[...].astype(BF16))


def _oproj(o, x, mods, gidx, w, wi, *, tm, tn, rows_per_batch, vmem_mib):
    M, D = x.shape
    return pl.pallas_call(
        _oproj_kernel,
        grid=(M // tm, D // tn),
        in_specs=[
            pl.BlockSpec((tm, D), lambda i, n: (i, 0)),
            pl.BlockSpec((None, D, tn), lambda i, n: (wi, 0, n)),
            pl.BlockSpec((tm, tn), lambda i, n: (i, n)),
            _mod_spec(mods, gidx, tm, rows_per_batch, tn, ncol=D // tn),
        ],
        out_specs=pl.BlockSpec((tm, tn), lambda i, n: (i, n)),
        out_shape=jax.ShapeDtypeStruct((M, D), F32),
        compiler_params=_params(("arbitrary", "arbitrary"), vmem_mib),
        name="oproj",
    )(o, w, x, mods)


def _pool_kernel(x_ref, g_ref, sh_ref, sc_ref, gt_ref, prev_ref, w_ref, ps_ref, y_ref, st_ref,
                 buf_ref, *, tt, pos0):
    t = pl.program_id(1)
    D = x_ref.shape[-1]
    group = D // len(POOL_WINDOWS)

    @pl.when(t == 0)
    def _():
        buf_ref[0:POOL_HALO, :] = prev_ref[...]

    @pl.when(t > 0)
    def _():
        buf_ref[0:POOL_HALO, :] = buf_ref[tt:tt + POOL_HALO, :]

    x = x_ref[...]
    buf_ref[POOL_HALO:POOL_HALO + tt, :] = _norm_mod(x, g_ref[...], sh_ref[...], sc_ref[...])
    st_ref[...] = buf_ref[tt:tt + POOL_HALO, :]

    pos = (pos0 + t * tt + lax.broadcasted_iota(jnp.int32, (tt, 1), 0)).astype(F32)
    for gi, w in enumerate(POOL_WINDOWS):
        sl = slice(gi * group, (gi + 1) * group)
        h = buf_ref[POOL_HALO:POOL_HALO + tt, sl]
        wsum = h
        for back in range(1, w):
            wsum = wsum + buf_ref[POOL_HALO - back:POOL_HALO - back + tt, sl]
        cnt = jnp.minimum(float(w), pos + 1.0)
        diff = (wsum / cnt - h).astype(BF16)
        m = _dot(diff, w_ref[gi].astype(BF16)) * ps_ref[:, sl]
        y_ref[:, sl] = x[:, sl] + gt_ref[:, sl] * m


def _pool(x3, mods_b, midx, g, prev, w_pool_l, pool_scale_l, pos0, *, tt):
    B, T, D = x3.shape
    ms = lambda j: pl.BlockSpec((None, 1, D), lambda b, t: (b, 0, j))
    return pl.pallas_call(
        functools.partial(_pool_kernel, tt=tt, pos0=pos0),
        grid=(B, T // tt),
        in_specs=[
            pl.BlockSpec((None, tt, D), lambda b, t: (b, t, 0)),
            pl.BlockSpec((1, D), lambda b, t: (0, 0)),
            ms(midx[0]), ms(midx[1]), ms(midx[2]),
            pl.BlockSpec((None, POOL_HALO, D), lambda b, t: (b, 0, 0)),
            pl.BlockSpec(w_pool_l.shape, lambda b, t: (0, 0, 0)),
            pl.BlockSpec((1, D), lambda b, t: (0, 0)),
        ],
        out_specs=[
            pl.BlockSpec((None, tt, D), lambda b, t: (b, t, 0)),
            pl.BlockSpec((None, POOL_HALO, D), lambda b, t: (b, 0, 0)),
        ],
        out_shape=[jax.ShapeDtypeStruct((B, T, D), F32),
                   jax.ShapeDtypeStruct((B, POOL_HALO, D), F32)],
        scratch_shapes=[pltpu.VMEM((POOL_HALO + tt, D), F32)],
        compiler_params=_params(("arbitrary", "arbitrary"), 48),
        name="pool",
    )(x3, g, mods_b, mods_b, mods_b, prev, w_pool_l, pool_scale_l.reshape(1, D))


LOG2E = 1.4426950408889634


def _softplus2(y):
    neg_abs = lax.bitcast_convert_type(
        lax.bitcast_convert_type(y, jnp.uint32) | jnp.uint32(0x80000000), F32)
    return jnp.maximum(y, 0.0) + jnp.log2(1.0 + jnp.exp2(neg_abs))


def _split_bf16(x, axis):
    hi = x.astype(BF16)
    lo = (x - hi.astype(F32)).astype(BF16)
    return jnp.concatenate([hi, lo], axis=axis)


def _attn_kernel(bias_ref, q_ref, k_ref, v_ref, o_ref, acc_ref, tot_ref, *, tq, hps, scale):
    hg = pl.program_id(1)
    i = pl.program_id(2)
    row = lax.broadcasted_iota(jnp.int32, (tq, tq), 0)
    col = lax.broadcasted_iota(jnp.int32, (tq, tq), 1)
    tri = jnp.where(row >= col, 1.0, 0.0).astype(BF16)
    tri2 = jnp.concatenate([tri, tri], axis=0)
    causal = col < row
    heads = [slice(c * HEAD_DIM, (c + 1) * HEAD_DIM) for c in range(hps)]
    bias = [bias_ref[hg * hps + c] * LOG2E for c in range(hps)]

    def tiles(k0, mask, first):
        zs = [_dot_nt(q_ref[:, hd], k_ref[pl.ds(k0, tq), hd]) * (scale * LOG2E) + b
              for hd, b in zip(heads, bias)]
        sps = [_softplus2(z) for z in zs]
        if mask is not None:
            sps = [jnp.where(mask, sp, 0.0) for sp in sps]
        sums = [_dot(_split_bf16(sp, 1), tri2) for sp in sps]
        probs = [jnp.exp2(z - sm) for z, sm in zip(zs, sums)]
        if mask is not None:
            probs = [jnp.where(mask, a, 0.0) for a in probs]
        pvs = [_dot(a.astype(BF16), v_ref[pl.ds(k0, tq), hd]) for a, hd in zip(probs, heads)]
        for c, (pv, sm) in enumerate(zip(pvs, sums)):
            tile_total = sm[:, :1]
            if first:
                acc_ref[c] = pv
                tot_ref[c] = tile_total
            else:
                total = tot_ref[c]
                acc_ref[c] += pv * jnp.exp2(-total)
                tot_ref[c] = total + tile_total

    tiles(pl.multiple_of(i * tq, tq), causal, True)

    def body(n, carry):
        tiles(pl.multiple_of((i - 1 - n) * tq, tq), None, False)
        return carry

    lax.fori_loop(0, i, body, 0)
    for c, hd in enumerate(heads):
        o_ref[:, hd] = acc_ref[c].astype(o_ref.dtype)


def _attn(q, k, v, bias, B, T, *, tq, hps):
    M, D = q.shape
    H = D // HEAD_DIM
    nq = T // tq
    w = hps * HEAD_DIM
    kv_spec = pl.BlockSpec((None, T, w), lambda b, h, i: (b, 0, h))
    return pl.pallas_call(
        functools.partial(_attn_kernel, tq=tq, hps=hps, scale=HEAD_DIM ** -0.5),
        grid=(B, H // hps, nq),
        in_specs=[
            pl.BlockSpec(memory_space=pltpu.SMEM),
            pl.BlockSpec((tq, w), lambda b, h, i: (b * nq + i, h)),
            kv_spec, kv_spec,
        ],
        out_specs=pl.BlockSpec((tq, w), lambda b, h, i: (b * nq + i, h)),
        out_shape=jax.ShapeDtypeStruct((M, D), BF16),
        scratch_shapes=[pltpu.VMEM((hps, tq, HEAD_DIM), F32), pltpu.VMEM((hps, tq, 1), F32)],
        compiler_params=_params(("arbitrary", "arbitrary", "arbitrary"), 32),
        name="attn",
    )(bias, q, k.reshape(B, T, D), v.reshape(B, T, D))


def _attn_paged_kernel(pt_ref, qbd_ref, bias_ref, kn_ref, vn_ref, *rest, n_pg, page, n_new,
                       n_heads, scale):
    k_refs, v_refs = rest[:n_pg], rest[n_pg:2 * n_pg]
    o_ref, acc_ref, tot_ref = rest[2 * n_pg:]
    s = pl.program_id(1)
    lanes = qbd_ref.shape[1]
    heads_per_group = MXU_BF16_ROWS // n_new
    row = lax.broadcasted_iota(jnp.int32, (page, page), 0)
    col = lax.broadcasted_iota(jnp.int32, (page, page), 1)
    tri = jnp.where(col >= row, 1.0, 0.0).astype(BF16)
    tri2 = jnp.concatenate([tri, tri], axis=1)

    head = lambda hd: slice(hd * HEAD_DIM, (hd + 1) * HEAD_DIM)
    def paged(ref):
        by_head = jnp.swapaxes(ref[...].astype(BF16).reshape(page, n_heads, HEAD_DIM), 0, 1)
        return lambda hd: by_head[hd]
    dense = lambda ref: (lambda hd: ref[:, head(hd)])

    def tiles(slabs, masked):
        zs = []
        for k_slab, _ in slabs:
            z = None
            for hd in range(n_heads):
                part = _dot(k_slab(hd), qbd_ref[head(hd), :])
                z = part if z is None else z + part
            zs.append(z * (scale * LOG2E) + bias_ref[...] * LOG2E)
        sps = [_softplus2(z) for z in zs]
        if masked:
            key = lax.broadcasted_iota(jnp.int32, (page, lanes), 0)
            qry = lax.broadcasted_iota(jnp.int32, (page, lanes), 1) % n_new
            mask = key < qry
            sps = [jnp.where(mask, sp, 0.0) for sp in sps]
        total = tot_ref[...]
        totals = []
        for sp in sps:
            totals.append(total)
            total = total + jnp.sum(sp, axis=0, keepdims=True)
        tot_ref[...] = total
        sums = [_dot(tri2, _split_bf16(sp, 0)) for sp in sps]
        probs = [jnp.exp2(z - sm - t) for z, sm, t in zip(zs, sums, totals)]
        if masked:
            probs = [jnp.where(mask, a, 0.0) for a in probs]
        pts = [a.T.astype(BF16) for a in probs]
        for hd in range(n_heads):
            g0 = (hd // heads_per_group) * MXU_BF16_ROWS
            pv = None
            for pt, (_, v_slab) in zip(pts, slabs):
                part = _dot(pt[g0:g0 + MXU_BF16_ROWS], v_slab(hd))
                pv = part if pv is None else pv + part
            acc_ref[hd] += pv

    @pl.when(s == 0)
    def _():
        acc_ref[...] = jnp.zeros_like(acc_ref)
        tot_ref[...] = jnp.zeros_like(tot_ref)
        tiles([(dense(kn_ref), dense(vn_ref))], True)

    tiles([(paged(k_refs[p]), paged(v_refs[p])) for p in range(n_pg)], False)

    @pl.when(s == pl.num_programs(1) - 1)
    def _():
        for hd in range(n_heads):
            r0 = (hd % heads_per_group) * n_new
            o_ref[:, head(hd)] = acc_ref[hd, r0:r0 + n_new, :]


def _attn_paged(q, k_new, v_new, cache_k, cache_v, page_table, bias, B, T, *, n_pg):
    M, D = q.shape
    n_phys, page, H, _ = cache_k.shape
    n_pages = page_table.shape[1]
    assert MXU_BF16_ROWS % T == 0 and n_pages % n_pg == 0
    n_qh = H * T
    lanes = V7X_LANES * pl.cdiv(n_qh, V7X_LANES)
    q4 = q.reshape(B, T, H, HEAD_DIM)
    qbd = jnp.einsum('bthd,hg->bhdgt', q4, jnp.eye(H, dtype=q.dtype)).reshape(B, D, n_qh)
    qbd = jnp.pad(qbd, ((0, 0), (0, 0), (0, lanes - n_qh)))
    bias_row = jnp.pad(jnp.repeat(bias.astype(F32), T), (0, lanes - n_qh)).reshape(1, lanes)
    pad_new = lambda a: jnp.pad(a.reshape(B, T, D), ((0, 0), (0, page - T), (0, 0)))

    ck = cache_k.reshape(n_phys, page * H, HEAD_DIM)
    cv = cache_v.reshape(n_phys, page * H, HEAD_DIM)

    def page_spec(p):
        return pl.BlockSpec((None, page * H, HEAD_DIM),
                            lambda b, s, pt: (pt[b, n_pages - 1 - (s * n_pg + p)], 0, 0))

    grid_spec = pltpu.PrefetchScalarGridSpec(
        num_scalar_prefetch=1,
        grid=(B, n_pages // n_pg),
        in_specs=[
            pl.BlockSpec((None, D, lanes), lambda b, s, pt: (b, 0, 0)),
            pl.BlockSpec((1, lanes), lambda b, s, pt: (0, 0)),
            pl.BlockSpec((None, page, D), lambda b, s, pt: (b, 0, 0)),
            pl.BlockSpec((None, page, D), lambda b, s, pt: (b, 0, 0)),
        ] + [page_spec(p) for p in range(n_pg)] * 2,
        out_specs=pl.BlockSpec((None, T, D), lambda b, s, pt: (b, 0, 0)),
        scratch_shapes=[pltpu.VMEM((H, MXU_BF16_ROWS, HEAD_DIM), F32),
                        pltpu.VMEM((1, lanes), F32)],
    )
    o = pl.pallas_call(
        functools.partial(_attn_paged_kernel, n_pg=n_pg, page=page, n_new=T, n_heads=H,
                          scale=HEAD_DIM ** -0.5),
        grid_spec=grid_spec,
        out_shape=jax.ShapeDtypeStruct((B, T, D), F32),
        compiler_params=_params(("arbitrary", "arbitrary"), 48),
        name="attn_paged",
    )(page_table, qbd, bias_row, pad_new(k_new), pad_new(v_new),
      *([ck] * n_pg), *([cv] * n_pg))
    return o.reshape(M, D).astype(BF16)


class _Group:
    def __init__(self, x3, mods_b, mods_kv_b, pos0, prev, past, *, tm, tf, tm_proj, tn_proj,
                 tn_oproj, tt, ffn_vmem):
        self.B, self.T, self.D = x3.shape
        self.x = x3.reshape(self.B * self.T, self.D)
        self.mods_b, self.mods_kv_b = mods_b, mods_kv_b
        self.pos0, self.prev, self.past = pos0, prev, past
        self.tt, self.tn_proj, self.tn_oproj = tt, tn_proj, tn_oproj
        self.ffn_tile = dict(tm=tm, rows_per_batch=self.T, tf=tf, vmem_mib=ffn_vmem)
        self.proj_tile = dict(tm=tm_proj, rows_per_batch=self.T, vmem_mib=58)
        self.per_row = self.T < tm
        self.states = []
        self.k16 = self.v16 = self.k32 = self.v32 = None

    def mods(self, m):
        return jnp.repeat(m.reshape(self.B, -1), self.T, axis=0) if self.per_row else m


def _ffn_both(main, side, l, j, midx, g, w_gate, w_up, w_down):
    n_tiles = main.x.shape[0] // main.ffn_tile["tm"]
    if n_tiles == side.B and side.T <= RIDER_ROWS:
        B, T, D = side.B, side.T, side.D
        xr = jnp.pad(side.x.reshape(B, T, D), ((0, 0), (0, RIDER_ROWS - T), (0, 0)))
        main.x, yr = _ffn(main.x, main.mods(main.mods_b[l]), midx, g, w_gate, w_up, w_down, l, j,
                          rider=(xr.reshape(B * RIDER_ROWS, D), side.mods_b[l]), **main.ffn_tile)
        side.x = yr.reshape(B, RIDER_ROWS, D)[:, :T].reshape(B * T, D)
    else:
        for gp in (main, side):
            gp.x = _ffn(gp.x, gp.mods(gp.mods_b[l]), midx, g, w_gate, w_up, w_down, l, j,
                        **gp.ffn_tile)


def _mixer(gp, l, wts):
    (norm_g, _, _, _, w_pool, pool_scale, _, _, _, w_q, q_norm_g, sb_bias, w_o) = wts
    B, T, D = gp.B, gp.T, gp.D
    n_a = w_pool.shape[0]
    g = norm_g[l, 1].reshape(1, D)
    if l < n_a:
        y3, st = _pool(gp.x.reshape(B, T, D), gp.mods_b[l], (3, 4, 5), g, gp.prev[l], w_pool[l],
                       pool_scale[l], gp.pos0, tt=gp.tt)
        gp.x = y3.reshape(B * T, D)
        gp.states.append(st[:, POOL_HALO - POOL_STATE:])
        return
    i = l - n_a
    mods = gp.mods(gp.mods_b[l])
    (q,) = _proj(gp.x, mods, (3, 4), g, w_q, i, 0, D, q_norm_g[i], (BF16,), tn=gp.tn_proj,
                 **gp.proj_tile)
    if gp.past is None:
        o = _attn(q, gp.k16, gp.v16, sb_bias[i], B, T, tq=256, hps=8)
    else:
        o = _attn_paged(q, gp.k16, gp.v16, *gp.past, sb_bias[i], B, T, n_pg=8)
    gp.x = _oproj(o, gp.x, mods, 5, w_o, i, tn=gp.tn_oproj, **gp.proj_tile)


def _shared_kv(gp, wts):
    (_, _, _, _, _, _, kv_norm_g, w_kv, k_norm_g, _, _, _, _) = wts
    D = gp.D
    mkv = gp.mods(gp.mods_kv_b)
    g = kv_norm_g.reshape(1, D)
    gp.k32, gp.k16 = _proj(gp.x, mkv, (0, 1), g, w_kv[None], 0, 0, D, k_norm_g, (F32, BF16),
                           tn=gp.tn_proj, **gp.proj_tile)
    gp.v32, gp.v16 = _proj(gp.x, mkv, (0, 1), g, w_kv[None], 0, D, D, None, (F32, BF16),
                           tn=gp.tn_proj, **gp.proj_tile)


def _trunks(main, side, wts):
    norm_g, w_gate, w_up, w_down, w_pool = wts[:5]
    depth, n_a, D = norm_g.shape[0], w_pool.shape[0], main.D
    for l in range(depth):
        _ffn_both(main, side, l, 0, (0, 1, 2), norm_g[l, 0].reshape(1, D), w_gate, w_up, w_down)
        for gp in (main, side):
            _mixer(gp, l, wts)
        _ffn_both(main, side, l, 1, (6, 7, 8), norm_g[l, 2].reshape(1, D), w_gate, w_up, w_down)
        if l == n_a - 1:
            for gp in (main, side):
                _shared_kv(gp, wts)
    H = D // HEAD_DIM
    heads = lambda gp, a: a.reshape(gp.B, gp.T, H, HEAD_DIM)
    return [(gp.x.reshape(gp.B, gp.T, D), heads(gp, gp.k32), heads(gp, gp.v32),
             jnp.stack(gp.states)) for gp in (main, side)]


def kernel(x_prompt, x_sample, c_prompt, c_sample, cache_k, cache_v, state_pool, page_table, norm_g, w_ada, b_ada, w_gate, w_up, w_down, w_pool, pool_scale, kv_norm_g, w_ada_kv, b_ada_kv, w_kv, k_norm_g, w_q, q_norm_g, sb_bias, w_o):
    Bp, Tp, D = x_prompt.shape
    Bs, Ts, _ = x_sample.shape
    n_a = w_pool.shape[0]
    past_len = page_table.shape[1] * cache_k.shape[1]

    c_all = jnp.concatenate([c_prompt, c_sample], axis=0)
    rows = V7X_SUBLANES * 2 * pl.cdiv(Bp + Bs, V7X_SUBLANES * 2)
    c_all = jnp.pad(c_all, ((0, rows - (Bp + Bs)), (0, 0)))
    mods = _ada(c_all, w_ada, b_ada, tn=1024)
    mods_kv = _ada(c_all, w_ada_kv[None], b_ada_kv[None], tn=1024)[0]
    split = lambda m, lo, n: m[..., lo:lo + n, :][..., None, :]

    wts = (norm_g, w_gate, w_up, w_down, w_pool, pool_scale, kv_norm_g, w_kv, k_norm_g, w_q,
           q_norm_g, sb_bias, w_o)

    zero_prev = jnp.zeros((n_a, Bp, POOL_HALO, D), x_prompt.dtype)
    prompt = _Group(x_prompt, split(mods, 0, Bp), split(mods_kv, 0, Bp), 0, zero_prev, None,
                    tm=1024, tf=256, tm_proj=2048, tn_proj=256, tn_oproj=512, tt=512, ffn_vmem=58)
    prev_s = jnp.pad(state_pool, ((0, 0), (0, 0), (POOL_HALO - POOL_STATE, 0), (0, 0)))
    sample = _Group(x_sample, split(mods, Bp, Bs), split(mods_kv, Bp, Bs), past_len, prev_s,
                    (cache_k, cache_v, page_table),
                    tm=Bs * Ts, tf=512, tm_proj=Bs * Ts, tn_proj=512, tn_oproj=512, tt=Ts,
                    ffn_vmem=48)
    (y_p, k_p, v_p, pool_p), (y_s, k_s, v_s, pool_s) = _trunks(prompt, sample, wts)
    return (y_p, y_s, k_p, v_p, k_s, v_s, pool_p, pool_s)
```

```python
import functools

import jax
import jax.numpy as jnp
from jax import lax
from jax.experimental import pallas as pl
from jax.experimental.pallas import tpu as pltpu

V7X_LANES = 128
V7X_SUBLANES = 8
V7X_VMEM_BYTES = 64 * 1024 * 1024

POOL_WINDOWS = (2, 4, 8, 16)
POOL_STATE = max(POOL_WINDOWS) - 1
POOL_HALO = POOL_STATE + 1
HEAD_DIM = 128
N_MOD = 9
EPS = 1e-6
MXU_BF16_ROWS = 16
RIDER_ROWS = MXU_BF16_ROWS
FFN_DOWN_COLS = 512

BF16 = jnp.bfloat16
F32 = jnp.float32


def _dot(a, b):
    return jnp.dot(a, b, preferred_element_type=F32)


def _dot_nt(a, b):
    return lax.dot_general(a, b, (((1,), (1,)), ((), ())), preferred_element_type=F32)


def _norm_mod(x, g, shift, scale):
    ms = jnp.mean(x * x, axis=-1, keepdims=True)
    return x * lax.rsqrt(ms + EPS) * (g * (1.0 + scale)) + shift


def _params(sem, vmem_mib):
    return pltpu.CompilerParams(dimension_semantics=sem,
                                vmem_limit_bytes=vmem_mib * 1024 * 1024)


def _mod_spec(mods, j, tm, rows_per_batch, width, ncol=1):
    if ncol == 1:
        col = lambda g: j
    else:
        col = lambda g: j * ncol + g[1]
    if mods.ndim == 3:
        tiles_per_batch = rows_per_batch // tm
        return pl.BlockSpec((None, 1, width), lambda *g: (g[0] // tiles_per_batch, 0, col(g)))
    return pl.BlockSpec((tm, width), lambda *g: (g[0], col(g)))


def _ada_kernel(c_ref, w_ref, b_ref, o_ref):
    c = c_ref[...]
    sc = (c * jax.nn.sigmoid(c)).astype(BF16)
    o_ref[...] = _dot(sc, w_ref[...].astype(BF16)) + b_ref[...]


def _ada(c, w, b, tn):
    L, D, N = w.shape
    R = c.shape[0]
    return pl.pallas_call(
        _ada_kernel,
        grid=(L, N // tn),
        in_specs=[
            pl.BlockSpec((R, D), lambda l, n: (0, 0)),
            pl.BlockSpec((None, D, tn), lambda l, n: (l, 0, n)),
            pl.BlockSpec((None, 1, tn), lambda l, n: (l, 0, n)),
        ],
        out_specs=pl.BlockSpec((None, R, tn), lambda l, n: (l, 0, n)),
        out_shape=jax.ShapeDtypeStruct((L, R, N), F32),
        compiler_params=_params(("arbitrary", "arbitrary"), 40),
        name="ada",
    )(c, w, b.reshape(L, 1, N))


def _ffn_kernel(*refs, nf, tf, last, n_parts, down_cols):
    xs = refs[:n_parts]
    g_ref = refs[n_parts]
    mods = [refs[n_parts + 1 + 3 * p:n_parts + 4 + 3 * p] for p in range(n_parts)]
    wg_ref, wu_ref, wd_ref = refs[4 * n_parts + 1:4 * n_parts + 4]
    outs = refs[4 * n_parts + 4:5 * n_parts + 4]
    h_ref = refs[5 * n_parts + 4]
    f = pl.program_id(1)
    rows, r0 = [], 0
    for x_ref in xs:
        rows.append(slice(r0, r0 + x_ref.shape[0]))
        r0 += x_ref.shape[0]
    D = wd_ref.shape[1]

    @pl.when(f == 0)
    def _():
        for x_ref, (sh_ref, sc_ref, _), o_ref, rs in zip(xs, mods, outs, rows):
            h_ref[rs, :] = _norm_mod(x_ref[...], g_ref[...], sh_ref[...], sc_ref[...]).astype(BF16)

    def step(width, first=False, final=False):
        h = h_ref[...]
        gate = _dot(h, wg_ref[:, :width].astype(BF16))
        up = _dot(h, wu_ref[:, :width].astype(BF16))
        a = (gate * jax.nn.sigmoid(gate) * up).astype(BF16)
        for c0 in range(0, D, down_cols):
            cs = slice(c0, c0 + down_cols)
            r = _dot(a, wd_ref[:width, cs].astype(BF16))
            for x_ref, (_, _, gt_ref), o_ref, rs in zip(xs, mods, outs, rows):
                if first:
                    o_ref[:, cs] = r[rs]
                elif final:
                    o_ref[:, cs] = x_ref[:, cs] + 0.5 * gt_ref[:, cs] * (o_ref[:, cs] + r[rs])
                else:
                    o_ref[:, cs] += r[rs]

    assert nf >= 2
    pl.when(f == 0)(lambda: step(tf, first=True))
    pl.when((f > 0) & (f < nf - 1))(lambda: step(tf))
    pl.when(f == nf - 1)(lambda: step(last, final=True))


def _ffn(x, mods, midx, g, w_gate, w_up, w_down, l, j, *, tm, tf, rows_per_batch, vmem_mib,
         rider=None):
    M, D = x.shape
    F = w_gate.shape[-1]
    nf = pl.cdiv(F, tf)
    last = F - (nf - 1) * tf
    ms = functools.partial(_mod_spec, mods, tm=tm, rows_per_batch=rows_per_batch, width=D)
    parts = [(x, pl.BlockSpec((tm, D), lambda i, f: (i, 0)), [ms(j=k) for k in midx], mods)]
    if rider is not None:
        xr, mods_r = rider
        assert xr.shape[0] == (M // tm) * RIDER_ROWS and mods_r.shape[0] == M // tm
        mr = lambda k: pl.BlockSpec((None, 1, D), lambda i, f: (i, 0, k))
        parts.append((xr, pl.BlockSpec((RIDER_ROWS, D), lambda i, f: (i, 0)),
                      [mr(k) for k in midx], mods_r))
    n_parts = len(parts)
    wspec = lambda shape, idx: pl.BlockSpec((None, None) + shape, idx)
    outs = pl.pallas_call(
        functools.partial(_ffn_kernel, nf=nf, tf=tf, last=last, n_parts=n_parts,
                          down_cols=min(D, FFN_DOWN_COLS)),
        grid=(M // tm, nf),
        in_specs=[p[1] for p in parts] + [pl.BlockSpec((1, D), lambda i, f: (0, 0))]
        + [spec for p in parts for spec in p[2]] + [
            wspec((D, tf), lambda i, f: (l, j, 0, f)),
            wspec((D, tf), lambda i, f: (l, j, 0, f)),
            wspec((tf, D), lambda i, f: (l, j, f, 0)),
        ],
        out_specs=[p[1] for p in parts],
        out_shape=[jax.ShapeDtypeStruct(p[0].shape, F32) for p in parts],
        scratch_shapes=[pltpu.VMEM((sum(p[1].block_shape[0] for p in parts), D), BF16)],
        compiler_params=_params(("arbitrary", "arbitrary"), vmem_mib),
        name="ffn",
    )(*[p[0] for p in parts], g, *[p[3] for p in parts for _ in range(3)], w_gate, w_up, w_down)
    return outs[0] if rider is None else tuple(outs)


def _proj_kernel(x_ref, g_ref, sh_ref, sc_ref, w_ref, hg_ref, *rest, head_norm, n_out):
    outs, h_ref = rest[:n_out], rest[n_out]
    n = pl.program_id(1)

    def columns():
        y = _dot(h_ref[...], w_ref[...].astype(BF16))
        if head_norm:
            hg = hg_ref[...]
            for c in range(y.shape[1] // HEAD_DIM):
                sl = slice(c * HEAD_DIM, (c + 1) * HEAD_DIM)
                yh = y[:, sl]
                ms = jnp.mean(yh * yh, axis=-1, keepdims=True)
                yn = yh * lax.rsqrt(ms + EPS) * hg
                for o in outs:
                    o[:, sl] = yn.astype(o.dtype)
        else:
            for o in outs:
                o[...] = y.astype(o.dtype)

    @pl.when(n == 0)
    def _():
        h_ref[...] = _norm_mod(x_ref[...], g_ref[...], sh_ref[...], sc_ref[...]).astype(BF16)
        columns()

    pl.when(n > 0)(columns)


def _proj(x, mods, midx, g, w, wi, col0, n_cols, head_g, out_dtypes, *, tm, tn, rows_per_batch,
          vmem_mib):
    M, D = x.shape
    ms = functools.partial(_mod_spec, mods, tm=tm, rows_per_batch=rows_per_batch, width=D)
    cb = col0 // tn
    head_norm = head_g is not None
    if head_g is None:
        head_g = jnp.ones((HEAD_DIM,), F32)
    outs = pl.pallas_call(
        functools.partial(_proj_kernel, head_norm=head_norm, n_out=len(out_dtypes)),
        grid=(M // tm, n_cols // tn),
        in_specs=[
            pl.BlockSpec((tm, D), lambda i, n: (i, 0)),
            pl.BlockSpec((1, D), lambda i, n: (0, 0)),
            ms(j=midx[0]), ms(j=midx[1]),
            pl.BlockSpec((None, D, tn), lambda i, n: (wi, 0, cb + n)),
            pl.BlockSpec((1, HEAD_DIM), lambda i, n: (0, 0)),
        ],
        out_specs=[pl.BlockSpec((tm, tn), lambda i, n: (i, n)) for _ in out_dtypes],
        out_shape=[jax.ShapeDtypeStruct((M, n_cols), dt) for dt in out_dtypes],
        scratch_shapes=[pltpu.VMEM((tm, D), BF16)],
        compiler_params=_params(("arbitrary", "arbitrary"), vmem_mib),
        name="proj",
    )(x, g, mods, mods, w, head_g.reshape(1, HEAD_DIM))
    return outs


def _oproj_kernel(o_ref, w_ref, x_ref, gt_ref, y_ref):
    y_ref[...] = x_ref[...] + gt_ref[...] * _dot(o_ref[...], w_ref[...].astype(BF16))


def _oproj(o, x, mods, gidx, w, wi, *, tm, tn, rows_per_batch, vmem_mib):
    M, D = x.shape
    return pl.pallas_call(
        _oproj_kernel,
        grid=(M // tm, D // tn),
        in_specs=[
            pl.BlockSpec((tm, D), lambda i, n: (i, 0)),
            pl.BlockSpec((None, D, tn), lambda i, n: (wi, 0, n)),
            pl.BlockSpec((tm, tn), lambda i, n: (i, n)),
            _mod_spec(mods, gidx, tm, rows_per_batch, tn, ncol=D // tn),
        ],
        out_specs=pl.BlockSpec((tm, tn), lambda i, n: (i, n)),
        out_shape=jax.ShapeDtypeStruct((M, D), F32),
        compiler_params=_params(("arbitrary", "arbitrary"), vmem_mib),
        name="oproj",
    )(o, w, x, mods)


def _pool_kernel(x_ref, g_ref, sh_ref, sc_ref, gt_ref, prev_ref, w_ref, ps_ref, y_ref, st_ref,
                 buf_ref, *, tt, pos0):
    t = pl.program_id(1)
    D = x_ref.shape[-1]
    group = D // len(POOL_WINDOWS)

    @pl.when(t == 0)
    def _():
        buf_ref[0:POOL_HALO, :] = prev_ref[...]

    @pl.when(t > 0)
    def _():
        buf_ref[0:POOL_HALO, :] = buf_ref[tt:tt + POOL_HALO, :]

    x = x_ref[...]
    buf_ref[POOL_HALO:POOL_HALO + tt, :] = _norm_mod(x, g_ref[...], sh_ref[...], sc_ref[...])
    st_ref[...] = buf_ref[tt:tt + POOL_HALO, :]

    pos = (pos0 + t * tt + lax.broadcasted_iota(jnp.int32, (tt, 1), 0)).astype(F32)
    for gi, w in enumerate(POOL_WINDOWS):
        sl = slice(gi * group, (gi + 1) * group)
        h = buf_ref[POOL_HALO:POOL_HALO + tt, sl]
        wsum = h
        for back in range(1, w):
            wsum = wsum + buf_ref[POOL_HALO - back:POOL_HALO - back + tt, sl]
        cnt = jnp.minimum(float(w), pos + 1.0)
        diff = (wsum / cnt - h).astype(BF16)
        m = _dot(diff, w_ref[gi].astype(BF16)) * ps_ref[:, sl]
        y_ref[:, sl] = x[:, sl] + gt_ref[:, sl] * m


def _pool(x3, mods_b, midx, g, prev, w_pool_l, pool_scale_l, pos0, *, tt):
    B, T, D = x3.shape
    ms = lambda j: pl.BlockSpec((None, 1, D), lambda b, t: (b, 0, j))
    return pl.pallas_call(
        functools.partial(_pool_kernel, tt=tt, pos0=pos0),
        grid=(B, T // tt),
        in_specs=[
            pl.BlockSpec((None, tt, D), lambda b, t: (b, t, 0)),
            pl.BlockSpec((1, D), lambda b, t: (0, 0)),
            ms(midx[0]), ms(midx[1]), ms(midx[2]),
            pl.BlockSpec((None, POOL_HALO, D), lambda b, t: (b, 0, 0)),
            pl.BlockSpec(w_pool_l.shape, lambda b, t: (0, 0, 0)),
            pl.BlockSpec((1, D), lambda b, t: (0, 0)),
        ],
        out_specs=[
            pl.BlockSpec((None, tt, D), lambda b, t: (b, t, 0)),
            pl.BlockSpec((None, POOL_HALO, D), lambda b, t: (b, 0, 0)),
        ],
        out_shape=[jax.ShapeDtypeStruct((B, T, D), F32),
                   jax.ShapeDtypeStruct((B, POOL_HALO, D), F32)],
        scratch_shapes=[pltpu.VMEM((POOL_HALO + tt, D), F32)],
        compiler_params=_params(("arbitrary", "arbitrary"), 48),
        name="pool",
    )(x3, g, mods_b, mods_b, mods_b, prev, w_pool_l, pool_scale_l.reshape(1, D))


LOG2E = 1.4426950408889634


def _softplus2(y):
    neg_abs = -jnp.abs(y)
    return jnp.maximum(y, 0.0) + jnp.log2(1.0 + jnp.exp2(neg_abs))


def _split_bf16(x, axis):
    hi = x.astype(BF16)
    lo = (x - hi.astype(F32)).astype(BF16)
    return jnp.concatenate([hi, lo], axis=axis)


def _attn_kernel(bias_ref, q_ref, k_ref, v_ref, o_ref, acc_ref, tot_ref, *, tq, hps, scale):
    hg = pl.program_id(1)
    i = pl.program_id(2)
    row = lax.broadcasted_iota(jnp.int32, (tq, tq), 0)
    col = lax.broadcasted_iota(jnp.int32, (tq, tq), 1)
    tri = jnp.where(row >= col, 1.0, 0.0).astype(BF16)
    tri2 = jnp.concatenate([tri, tri], axis=0)
    causal = col < row
    heads = [slice(c * HEAD_DIM, (c + 1) * HEAD_DIM) for c in range(hps)]
    bias = [bias_ref[hg * hps + c] * LOG2E for c in range(hps)]

    def tiles(k0, mask, first):
        zs = [_dot_nt(q_ref[:, hd], k_ref[pl.ds(k0, tq), hd]) * (scale * LOG2E) + b
              for hd, b in zip(heads, bias)]
        sps = [_softplus2(z) for z in zs]
        if mask is not None:
            sps = [jnp.where(mask, sp, 0.0) for sp in sps]
        sums = [_dot(_split_bf16(sp, 1), tri2) for sp in sps]
        probs = [jnp.exp2(z - sm) for z, sm in zip(zs, sums)]
        if mask is not None:
            probs = [jnp.where(mask, a, 0.0) for a in probs]
        pvs = [_dot(a.astype(BF16), v_ref[pl.ds(k0, tq), hd]) for a, hd in zip(probs, heads)]
        for c, (pv, sm) in enumerate(zip(pvs, sums)):
            tile_total = sm[:, :1]
            if first:
                acc_ref[c] = pv
                tot_ref[c] = tile_total
            else:
                total = tot_ref[c]
                acc_ref[c] += pv * jnp.exp2(-total)
                tot_ref[c] = total + tile_total

    tiles(pl.multiple_of(i * tq, tq), causal, True)

    def body(n, carry):
        tiles(pl.multiple_of((i - 1 - n) * tq, tq), None, False)
        return carry

    lax.fori_loop(0, i, body, 0)
    for c, hd in enumerate(heads):
        o_ref[:, hd] = acc_ref[c].astype(o_ref.dtype)


def _attn(q, k, v, bias, B, T, *, tq, hps):
    M, D = q.shape
    H = D // HEAD_DIM
    nq = T // tq
    w = hps * HEAD_DIM
    kv_spec = pl.BlockSpec((None, T, w), lambda b, h, i: (b, 0, h))
    return pl.pallas_call(
        functools.partial(_attn_kernel, tq=tq, hps=hps, scale=HEAD_DIM ** -0.5),
        grid=(B, H // hps, nq),
        in_specs=[
            pl.BlockSpec(memory_space=pltpu.SMEM),
            pl.BlockSpec((tq, w), lambda b, h, i: (b * nq + i, h)),
            kv_spec, kv_spec,
        ],
        out_specs=pl.BlockSpec((tq, w), lambda b, h, i: (b * nq + i, h)),
        out_shape=jax.ShapeDtypeStruct((M, D), BF16),
        scratch_shapes=[pltpu.VMEM((hps, tq, HEAD_DIM), F32), pltpu.VMEM((hps, tq, 1), F32)],
        compiler_params=_params(("arbitrary", "arbitrary", "arbitrary"), 32),
        name="attn",
    )(bias, q, k.reshape(B, T, D), v.reshape(B, T, D))


def _attn_paged_kernel(pt_ref, qbd_ref, bias_ref, kn_ref, vn_ref, *rest, n_pg, page, n_new,
                       n_heads, scale):
    k_refs, v_refs = rest[:n_pg], rest[n_pg:2 * n_pg]
    o_ref, acc_ref, tot_ref = rest[2 * n_pg:]
    s = pl.program_id(1)
    lanes = qbd_ref.shape[1]
    heads_per_group = MXU_BF16_ROWS // n_new
    row = lax.broadcasted_iota(jnp.int32, (page, page), 0)
    col = lax.broadcasted_iota(jnp.int32, (page, page), 1)
    tri = jnp.where(col >= row, 1.0, 0.0).astype(BF16)
    tri2 = jnp.concatenate([tri, tri], axis=1)

    head = lambda hd: slice(hd * HEAD_DIM, (hd + 1) * HEAD_DIM)
    def paged(ref):
        by_head = jnp.swapaxes(ref[...].astype(BF16).reshape(page, n_heads, HEAD_DIM), 0, 1)
        return lambda hd: by_head[hd]
    dense = lambda ref: (lambda hd: ref[:, head(hd)])

    def tiles(slabs, masked):
        zs = []
        for k_slab, _ in slabs:
            z = None
            for hd in range(n_heads):
                part = _dot(k_slab(hd), qbd_ref[head(hd), :])
                z = part if z is None else z + part
            zs.append(z * (scale * LOG2E) + bias_ref[...] * LOG2E)
        sps = [_softplus2(z) for z in zs]
        if masked:
            key = lax.broadcasted_iota(jnp.int32, (page, lanes), 0)
            qry = lax.broadcasted_iota(jnp.int32, (page, lanes), 1) % n_new
            mask = key < qry
            sps = [jnp.where(mask, sp, 0.0) for sp in sps]
        total = tot_ref[...]
        totals = []
        for sp in sps:
            totals.append(total)
            total = total + jnp.sum(sp, axis=0, keepdims=True)
        tot_ref[...] = total
        sums = [_dot(tri2, _split_bf16(sp, 0)) for sp in sps]
        probs = [jnp.exp2(z - sm - t) for z, sm, t in zip(zs, sums, totals)]
        if masked:
            probs = [jnp.where(mask, a, 0.0) for a in probs]
        pts = [a.T.astype(BF16) for a in probs]
        for hd in range(n_heads):
            g0 = (hd // heads_per_group) * MXU_BF16_ROWS
            pv = None
            for pt, (_, v_slab) in zip(pts, slabs):
                part = _dot(pt[g0:g0 + MXU_BF16_ROWS], v_slab(hd))
                pv = part if pv is None else pv + part
            acc_ref[hd] += pv

    @pl.when(s == 0)
    def _():
        acc_ref[...] = jnp.zeros_like(acc_ref)
        tot_ref[...] = jnp.zeros_like(tot_ref)
        tiles([(dense(kn_ref), dense(vn_ref))], True)

    tiles([(paged(k_refs[p]), paged(v_refs[p])) for p in range(n_pg)], False)

    @pl.when(s == pl.num_programs(1) - 1)
    def _():
        for hd in range(n_heads):
            r0 = (hd % heads_per_group) * n_new
            o_ref[:, head(hd)] = acc_ref[hd, r0:r0 + n_new, :]


def _attn_paged(q, k_new, v_new, cache_k, cache_v, page_table, bias, B, T, *, n_pg):
    M, D = q.shape
    n_phys, page, H, _ = cache_k.shape
    n_pages = page_table.shape[1]
    assert MXU_BF16_ROWS % T == 0 and n_pages % n_pg == 0
    n_qh = H * T
    lanes = V7X_LANES * pl.cdiv(n_qh, V7X_LANES)
    q4 = q.reshape(B, T, H, HEAD_DIM)
    qbd = jnp.einsum('bthd,hg->bhdgt', q4, jnp.eye(H, dtype=q.dtype)).reshape(B, D, n_qh)
    qbd = jnp.pad(qbd, ((0, 0), (0, 0), (0, lanes - n_qh)))
    bias_row = jnp.pad(jnp.repeat(bias.astype(F32), T), (0, lanes - n_qh)).reshape(1, lanes)
    pad_new = lambda a: jnp.pad(a.reshape(B, T, D), ((0, 0), (0, page - T), (0, 0)))

    ck = cache_k.reshape(n_phys, page * H, HEAD_DIM)
    cv = cache_v.reshape(n_phys, page * H, HEAD_DIM)

    def page_spec(p):
        return pl.BlockSpec((None, page * H, HEAD_DIM),
                            lambda b, s, pt: (pt[b, n_pages - 1 - (s * n_pg + p)], 0, 0))

    grid_spec = pltpu.PrefetchScalarGridSpec(
        num_scalar_prefetch=1,
        grid=(B, n_pages // n_pg),
        in_specs=[
            pl.BlockSpec((None, D, lanes), lambda b, s, pt: (b, 0, 0)),
            pl.BlockSpec((1, lanes), lambda b, s, pt: (0, 0)),
            pl.BlockSpec((None, page, D), lambda b, s, pt: (b, 0, 0)),
            pl.BlockSpec((None, page, D), lambda b, s, pt: (b, 0, 0)),
        ] + [page_spec(p) for p in range(n_pg)] * 2,
        out_specs=pl.BlockSpec((None, T, D), lambda b, s, pt: (b, 0, 0)),
        scratch_shapes=[pltpu.VMEM((H, MXU_BF16_ROWS, HEAD_DIM), F32),
                        pltpu.VMEM((1, lanes), F32)],
    )
    o = pl.pallas_call(
        functools.partial(_attn_paged_kernel, n_pg=n_pg, page=page, n_new=T, n_heads=H,
                          scale=HEAD_DIM ** -0.5),
        grid_spec=grid_spec,
        out_shape=jax.ShapeDtypeStruct((B, T, D), F32),
        compiler_params=_params(("arbitrary", "arbitrary"), 48),
        name="attn_paged",
    )(page_table, qbd, bias_row, pad_new(k_new), pad_new(v_new),
      *([ck] * n_pg), *([cv] * n_pg))
    return o.reshape(M, D).astype(BF16)


class _Group:
    def __init__(self, x3, mods_b, mods_kv_b, pos0, prev, past, *, tm, tf, tm_proj, tn_proj,
                 tn_oproj, tt, ffn_vmem):
        self.B, self.T, self.D = x3.shape
        self.x = x3.reshape(self.B * self.T, self.D)
        self.mods_b, self.mods_kv_b = mods_b, mods_kv_b
        self.pos0, self.prev, self.past = pos0, prev, past
        self.tt, self.tn_proj, self.tn_oproj = tt, tn_proj, tn_oproj
        self.ffn_tile = dict(tm=tm, rows_per_batch=self.T, tf=tf, vmem_mib=ffn_vmem)
        self.proj_tile = dict(tm=tm_proj, rows_per_batch=self.T, vmem_mib=58)
        self.per_row = self.T < tm
        self.states = []
        self.k16 = self.v16 = self.k32 = self.v32 = None

    def mods(self, m):
        return jnp.repeat(m.reshape(self.B, -1), self.T, axis=0) if self.per_row else m


def _ffn_both(main, side, l, j, midx, g, w_gate, w_up, w_down):
    n_tiles = main.x.shape[0] // main.ffn_tile["tm"]
    if n_tiles == side.B and side.T <= RIDER_ROWS:
        B, T, D = side.B, side.T, side.D
        xr = jnp.pad(side.x.reshape(B, T, D), ((0, 0), (0, RIDER_ROWS - T), (0, 0)))
        main.x, yr = _ffn(main.x, main.mods(main.mods_b[l]), midx, g, w_gate, w_up, w_down, l, j,
                          rider=(xr.reshape(B * RIDER_ROWS, D), side.mods_b[l]), **main.ffn_tile)
        side.x = yr.reshape(B, RIDER_ROWS, D)[:, :T].reshape(B * T, D)
    else:
        for gp in (main, side):
            gp.x = _ffn(gp.x, gp.mods(gp.mods_b[l]), midx, g, w_gate, w_up, w_down, l, j,
                        **gp.ffn_tile)


def _mixer(gp, l, wts):
    (norm_g, _, _, _, w_pool, pool_scale, _, _, _, w_q, q_norm_g, sb_bias, w_o) = wts
    B, T, D = gp.B, gp.T, gp.D
    n_a = w_pool.shape[0]
    g = norm_g[l, 1].reshape(1, D)
    if l < n_a:
        y3, st = _pool(gp.x.reshape(B, T, D), gp.mods_b[l], (3, 4, 5), g, gp.prev[l], w_pool[l],
                       pool_scale[l], gp.pos0, tt=gp.tt)
        gp.x = y3.reshape(B * T, D)
        gp.states.append(st[:, POOL_HALO - POOL_STATE:])
        return
    i = l - n_a
    mods = gp.mods(gp.mods_b[l])
    (q,) = _proj(gp.x, mods, (3, 4), g, w_q, i, 0, D, q_norm_g[i], (BF16,), tn=gp.tn_proj,
                 **gp.proj_tile)
    if gp.past is None:
        o = _attn(q, gp.k16, gp.v16, sb_bias[i], B, T, tq=256, hps=8)
    else:
        o = _attn_paged(q, gp.k16, gp.v16, *gp.past, sb_bias[i], B, T, n_pg=8)
    gp.x = _oproj(o, gp.x, mods, 5, w_o, i, tn=gp.tn_oproj, **gp.proj_tile)


def _shared_kv(gp, wts):
    (_, _, _, _, _, _, kv_norm_g, w_kv, k_norm_g, _, _, _, _) = wts
    D = gp.D
    mkv = gp.mods(gp.mods_kv_b)
    g = kv_norm_g.reshape(1, D)
    gp.k32, gp.k16 = _proj(gp.x, mkv, (0, 1), g, w_kv[None], 0, 0, D, k_norm_g, (F32, BF16),
                           tn=gp.tn_proj, **gp.proj_tile)
    gp.v32, gp.v16 = _proj(gp.x, mkv, (0, 1), g, w_kv[None], 0, D, D, None, (F32, BF16),
                           tn=gp.tn_proj, **gp.proj_tile)


def _trunks(main, side, wts):
    norm_g, w_gate, w_up, w_down, w_pool = wts[:5]
    depth, n_a, D = norm_g.shape[0], w_pool.shape[0], main.D
    for l in range(depth):
        _ffn_both(main, side, l, 0, (0, 1, 2), norm_g[l, 0].reshape(1, D), w_gate, w_up, w_down)
        for gp in (main, side):
            _mixer(gp, l, wts)
        _ffn_both(main, side, l, 1, (6, 7, 8), norm_g[l, 2].reshape(1, D), w_gate, w_up, w_down)
        if l == n_a - 1:
            for gp in (main, side):
                _shared_kv(gp, wts)
    H = D // HEAD_DIM
    heads = lambda gp, a: a.reshape(gp.B, gp.T, H, HEAD_DIM)
    return [(gp.x.reshape(gp.B, gp.T, D), heads(gp, gp.k32), heads(gp, gp.v32),
             jnp.stack(gp.states)) for gp in (main, side)]


def kernel(x_prompt, x_sample, c_prompt, c_sample, cache_k, cache_v, state_pool, page_table, norm_g, w_ada, b_ada, w_gate, w_up, w_down, w_pool, pool_scale, kv_norm_g, w_ada_kv, b_ada_kv, w_kv, k_norm_g, w_q, q_norm_g, sb_bias, w_o):
    Bp, Tp, D = x_prompt.shape
    Bs, Ts, _ = x_sample.shape
    n_a = w_pool.shape[0]
    past_len = page_table.shape[1] * cache_k.shape[1]

    c_all = jnp.concatenate([c_prompt, c_sample], axis=0)
    rows = V7X_SUBLANES * 2 * pl.cdiv(Bp + Bs, V7X_SUBLANES * 2)
    c_all = jnp.pad(c_all, ((0, rows - (Bp + Bs)), (0, 0)))
    mods = _ada(c_all, w_ada, b_ada, tn=1024)
    mods_kv = _ada(c_all, w_ada_kv[None], b_ada_kv[None], tn=1024)[0]
    split = lambda m, lo, n: m[..., lo:lo + n, :][..., None, :]

    wts = (norm_g, w_gate, w_up, w_down, w_pool, pool_scale, kv_norm_g, w_kv, k_norm_g, w_q,
           q_norm_g, sb_bias, w_o)

    zero_prev = jnp.zeros((n_a, Bp, POOL_HALO, D), x_prompt.dtype)
    prompt = _Group(x_prompt, split(mods, 0, Bp), split(mods_kv, 0, Bp), 0, zero_prev, None,
                    tm=1024, tf=256, tm_proj=2048, tn_proj=256, tn_oproj=512, tt=512, ffn_vmem=58)
    prev_s = jnp.pad(state_pool, ((0, 0), (0, 0), (POOL_HALO - POOL_STATE, 0), (0, 0)))
    sample = _Group(x_sample, split(mods, Bp, Bs), split(mods_kv, Bp, Bs), past_len, prev_s,
                    (cache_k, cache_v, page_table),
                    tm=Bs * Ts, tf=512, tm_proj=Bs * Ts, tn_proj=512, tn_oproj=512, tt=Ts,
                    ffn_vmem=48)
    (y_p, k_p, v_p, pool_p), (y_s, k_s, v_s, pool_s) = _trunks(prompt, sample, wts)
    return (y_p, y_s, k_p, v_p, k_s, v_s, pool_p, pool_s)
```
